```python
import math
import jax, jax.numpy as jnp
from jax import lax
import numpy as np

D_MODEL = 2048
BATCH = 2
SEQ = 16384
DEPTH = 2

CHUNK = 64
MIX_WIDTH = D_MODEL
GROUP_WIDTH = MIX_WIDTH // 4

GM_BLOCK = 128
GM_HEADS = 4
GM_DIM = GROUP_WIDTH // GM_HEADS
SSM_HEADDIM = 64
SSM_HEADS = GROUP_WIDTH // SSM_HEADDIM
SSM_GROUPS = 2
SSM_STATE = 128
SSM_CONV = 4
SSM_CHUNK = CHUNK
ML_HEADS = 4
ML_DIM = GROUP_WIDTH // ML_HEADS
ML_CHUNK = CHUNK
ATT_HEADS = 4
ATT_DIM = GROUP_WIDTH // ATT_HEADS
Q_RANK = 384
IDX_HEADS = 8
IDX_DIM = 64
IDX_TOPK_MAX = 256
QBLOCK = 128
ROPE_THETA = 500000.0
N_EXPERTS = 32
TOP_K = 4
D_FF = D_MODEL
SWIGLU_LIMIT = 7.0
SWIGLU_ALPHA = 1.702
DEEPNORM_ALPHA = (2.0 * DEPTH) ** 0.25
DEEPNORM_BETA = (8.0 * DEPTH) ** -0.25

GM_COLS = 2 * GROUP_WIDTH
SSM_XBC = GROUP_WIDTH + 2 * SSM_GROUPS * SSM_STATE
SSM_COLS = GROUP_WIDTH + SSM_XBC + SSM_HEADS
ML_COLS = 4 * GROUP_WIDTH + 2 * ML_HEADS
ATT_COLS = Q_RANK + 2 * GROUP_WIDTH + IDX_DIM + IDX_HEADS
IN_COLS = GM_COLS + SSM_COLS + ML_COLS + ATT_COLS

kernel_name = 'hybrid_streaming_encoder_block'


def layer_norm(x, g, b=None, eps=1e-5):
    x32 = x.astype(jnp.float32)
    mu = jnp.mean(x32, axis=-1, keepdims=True)
    var = jnp.mean(jnp.square(x32 - mu), axis=-1, keepdims=True)
    y = (x32 - mu) * lax.rsqrt(var + eps) * g.astype(jnp.float32)
    if b is not None:
        y = y + b.astype(jnp.float32)
    return y.astype(x.dtype)


def rms_norm(x, g, eps=1e-6):
    x32 = x.astype(jnp.float32)
    y = x32 * lax.rsqrt(jnp.mean(jnp.square(x32), axis=-1, keepdims=True) + eps)
    return (y * g.astype(jnp.float32)).astype(x.dtype)


def partial_rotary(x, positions):
    rot = x.shape[-1] // 4
    half = rot // 2
    inv_freq = ROPE_THETA ** (-jnp.arange(half, dtype=jnp.float32) * 2.0 / rot)
    ang = positions.astype(jnp.float32)[:, :, None] * inv_freq
    cos = jnp.cos(ang)[:, :, None, :]
    sin = jnp.sin(ang)[:, :, None, :]
    xr = x[..., :rot].astype(jnp.float32)
    x1, x2 = xr[..., :half], xr[..., half:]
    rotated = jnp.concatenate([x1 * cos - x2 * sin, x2 * cos + x1 * sin], axis=-1)
    return jnp.concatenate([rotated.astype(x.dtype), x[..., rot:]], axis=-1)


def gmlp_mixer(h, ln_g, ln_b, w_s, b_s):
    bsz, seq, _ = h.shape
    h = jax.nn.gelu(h, approximate=False)
    u, v = jnp.split(h, 2, axis=-1)
    nb = seq // GM_BLOCK
    u = u.reshape(bsz, nb, GM_BLOCK, GM_HEADS, GM_DIM)
    v = layer_norm(v.reshape(bsz, nb, GM_BLOCK, GM_HEADS, GM_DIM), ln_g, ln_b)
    pos_chunk = jnp.arange(GM_BLOCK) // CHUNK
    allowed = pos_chunk[None, :] <= pos_chunk[:, None]
    w = jnp.where(allowed[None], w_s, jnp.zeros((), w_s.dtype))
    v_mix = jnp.einsum('gts,bnsgd->bntgd', w, v) + b_s.T[None, None, :, :, None]
    return (u * v_mix).reshape(bsz, seq, GROUP_WIDTH)


def causal_depthwise_conv(x, w, b):
    ch = x.shape[-1]
    y = lax.conv_general_dilated(x, w[:, None, :].astype(x.dtype), window_strides=(1,),
                                 padding=[(SSM_CONV - 1, 0)],
                                 dimension_numbers=('NWC', 'WIO', 'NWC'),
                                 feature_group_count=ch)
    return y + b


def mamba2_mixer(h, conv_w, conv_b, dt_bias, a_log, d_skip, norm_g):
    f32 = jnp.float32
    bsz, seq, _ = h.shape
    z, xbc, dt = jnp.split(h, [GROUP_WIDTH, GROUP_WIDTH + SSM_XBC], axis=-1)
    xbc = jax.nn.silu(causal_depthwise_conv(xbc, conv_w, conv_b))
    xs, bm, cm = jnp.split(xbc, [GROUP_WIDTH, GROUP_WIDTH + SSM_GROUPS * SSM_STATE], axis=-1)
    nc = seq // SSM_CHUNK
    rep = SSM_HEADS // SSM_GROUPS
    xs = xs.astype(f32).reshape(bsz, nc, SSM_CHUNK, SSM_HEADS, SSM_HEADDIM)
    bm = jnp.repeat(bm.astype(f32).reshape(bsz, nc, SSM_CHUNK, SSM_GROUPS, SSM_STATE), rep, axis=3)
    cm = jnp.repeat(cm.astype(f32).reshape(bsz, nc, SSM_CHUNK, SSM_GROUPS, SSM_STATE), rep, axis=3)
    dt = jax.nn.softplus(dt.astype(f32) + dt_bias.astype(f32)).reshape(bsz, nc, SSM_CHUNK, SSM_HEADS)
    a = -jnp.exp(a_log.astype(f32))
    da = jnp.swapaxes(dt * a, 2, 3)
    a_cs = jnp.cumsum(da, axis=-1)
    xdt = xs * dt[..., None]
    idx = jnp.arange(SSM_CHUNK)
    tri = idx[:, None] >= idx[None, :]
    decay_mat = jnp.exp(jnp.where(tri, a_cs[..., :, None] - a_cs[..., None, :], -jnp.inf))
    scores = jnp.einsum('bclhn,bcshn->bchls', cm, bm) * decay_mat
    y_diag = jnp.einsum('bchls,bcshp->bclhp', scores, xdt)
    decay_states = jnp.exp(a_cs[..., -1:] - a_cs)
    states = jnp.einsum('bclhn,bchl,bclhp->bchpn', bm, decay_states, xdt)
    chunk_decay = jnp.exp(a_cs[..., -1])

    def step(s, inp):
        dec, st = inp
        return dec[..., None, None] * s + st, s

    s0 = jnp.zeros((bsz, SSM_HEADS, SSM_HEADDIM, SSM_STATE), f32)
    _, prev = lax.scan(step, s0, (jnp.moveaxis(chunk_decay, 1, 0), jnp.moveaxis(states, 1, 0)))
    prev = jnp.moveaxis(prev, 0, 1)
    y_off = jnp.einsum('bclhn,bchpn->bclhp', cm, prev) * jnp.swapaxes(jnp.exp(a_cs), 2, 3)[..., None]
    y = y_diag + y_off + xs * d_skip.astype(f32)[:, None]
    y = y.reshape(bsz, seq, GROUP_WIDTH) * jax.nn.silu(z.astype(f32))
    return rms_norm(y, norm_g).astype(h.dtype)


def mlstm_mixer(h, b_i, b_f, norm_g):
    f32 = jnp.float32
    bsz, seq, _ = h.shape
    w = GROUP_WIDTH
    q, k, v, o, ig, fg = jnp.split(h, [w, 2 * w, 3 * w, 4 * w, 4 * w + ML_HEADS], axis=-1)
    nc = seq // ML_CHUNK
    shp = (bsz, nc, ML_CHUNK, ML_HEADS, ML_DIM)
    q = q.astype(f32).reshape(shp)
    k = k.astype(f32).reshape(shp) * (ML_DIM ** -0.5)
    v = v.astype(f32).reshape(shp)
    i_t = (ig.astype(f32) + b_i.astype(f32)).reshape(bsz, nc, ML_CHUNK, ML_HEADS)
    log_f = jax.nn.log_sigmoid(fg.astype(f32) + b_f.astype(f32)).reshape(bsz, nc, ML_CHUNK, ML_HEADS)
    b = jnp.cumsum(log_f, axis=2)
    b_end = b[:, :, -1, :]
    a = b_end[:, :, None, :] - b + i_t

    def step(carry, inp):
        c_st, n_st, m_st = carry
        be, a_c, k_c, v_c = inp
        m_new = jnp.maximum(be + m_st, jnp.max(a_c, axis=1))
        decay = jnp.exp(be + m_st - m_new)
        wgt = jnp.exp(a_c - m_new[:, None, :])
        c_new = decay[..., None, None] * c_st + jnp.einsum('blh,blhv,blhk->bhvk', wgt, v_c, k_c)
        n_new = decay[..., None] * n_st + jnp.einsum('blh,blhk->bhk', wgt, k_c)
        return (c_new, n_new, m_new), (c_st, n_st, m_st)

    init = (jnp.zeros((bsz, ML_HEADS, ML_DIM, ML_DIM), f32),
            jnp.zeros((bsz, ML_HEADS, ML_DIM), f32),
            jnp.zeros((bsz, ML_HEADS), f32))
    _, (c_prev, n_prev, m_prev) = lax.scan(
        step, init, (jnp.moveaxis(b_end, 1, 0), jnp.moveaxis(a, 1, 0),
                     jnp.moveaxis(k, 1, 0), jnp.moveaxis(v, 1, 0)))
    c_prev = jnp.moveaxis(c_prev, 0, 1)
    n_prev = jnp.moveaxis(n_prev, 0, 1)
    m_prev = jnp.moveaxis(m_prev, 0, 1)
    b_t = jnp.swapaxes(b, 2, 3)
    i_tt = jnp.swapaxes(i_t, 2, 3)
    idx = jnp.arange(ML_CHUNK)
    tri = idx[:, None] >= idx[None, :]
    log_d = jnp.where(tri, b_t[..., :, None] - b_t[..., None, :] + i_tt[..., None, :], -jnp.inf)
    inter = b_t + m_prev[..., None]
    m_j = jnp.maximum(jnp.max(log_d, axis=-1), inter)
    s_mat = jnp.exp(log_d - m_j[..., None]) * jnp.einsum('bclhd,bcshd->bchls', q, k)
    g = jnp.exp(inter - m_j)
    num = (jnp.einsum('bchls,bcshd->bclhd', s_mat, v)
           + jnp.einsum('bchl,bclhk,bchvk->bclhv', g, q, c_prev))
    den = jnp.sum(s_mat, axis=-1) + g * jnp.einsum('bclhk,bchk->bchl', q, n_prev)
    denom = jnp.maximum(jnp.abs(den), jnp.exp(-m_j))
    hh = num / jnp.swapaxes(denom, 2, 3)[..., None]
    hh = layer_norm(hh.reshape(bsz, seq, ML_HEADS, ML_DIM), norm_g)
    return (hh.reshape(bsz, seq, w) * jax.nn.sigmoid(o.astype(f32))).astype(h.dtype)


def dsa_mixer(h, positions, q_norm_g, w_uq, idxk_g, idxk_b):
    f32 = jnp.float32
    bsz, seq, _ = h.shape
    w = GROUP_WIDTH
    c_q, k, v, k_idx, w_idx = jnp.split(
        h, [Q_RANK, Q_RANK + w, Q_RANK + 2 * w, Q_RANK + 2 * w + IDX_DIM], axis=-1)
    q_all = jnp.einsum('bsr,re->bse', rms_norm(c_q, q_norm_g), w_uq)
    q, q_idx = jnp.split(q_all, [w], axis=-1)
    q = partial_rotary(q.reshape(bsz, seq, ATT_HEADS, ATT_DIM), positions)
    k = partial_rotary(k.reshape(bsz, seq, ATT_HEADS, ATT_DIM), positions)
    v = v.reshape(bsz, seq, ATT_HEADS, ATT_DIM)
    q_idx = partial_rotary(q_idx.reshape(bsz, seq, IDX_HEADS, IDX_DIM), positions)
    k_idx = partial_rotary(layer_norm(k_idx, idxk_g, idxk_b)[:, :, None, :], positions)[:, :, 0, :]
    w_idx = w_idx.astype(f32) * (IDX_HEADS ** -0.5 * IDX_DIM ** -0.5)
    top_k = min(IDX_TOPK_MAX, seq // 4)
    key_chunk = jnp.arange(seq) // CHUNK

    def block(i):
        start = i * QBLOCK
        qi = lax.dynamic_slice_in_dim(q_idx, start, QBLOCK, axis=1)
        wi = lax.dynamic_slice_in_dim(w_idx, start, QBLOCK, axis=1)
        qa = lax.dynamic_slice_in_dim(q, start, QBLOCK, axis=1)
        rel = jax.nn.relu(jnp.einsum('bthd,bsd->bths', qi, k_idx).astype(f32))
        score = jnp.einsum('bths,bth->bts', rel, wi)
        q_chunk = (start + jnp.arange(QBLOCK)) // CHUNK
        allowed = key_chunk[None, :] <= q_chunk[:, None]
        score = jnp.where(allowed[None], score, -jnp.inf)
        _, sel = lax.top_k(score, top_k)
        valid = (sel // CHUNK) <= q_chunk[None, :, None]
        kg = jax.vmap(lambda kb, ib: kb[ib])(k, sel)
        vg = jax.vmap(lambda vb, ib: vb[ib])(v, sel)
        logits = jnp.einsum('bthd,btjhd->bhtj', qa, kg).astype(f32) * (ATT_DIM ** -0.5)
        logits = jnp.where(valid[:, None], logits, -jnp.inf)
        p = jax.nn.softmax(logits, axis=-1).astype(vg.dtype)
        return jnp.einsum('bhtj,btjhd->bthd', p, vg)

    out = lax.map(block, jnp.arange(seq // QBLOCK))
    return jnp.moveaxis(out, 0, 1).reshape(bsz, seq, w)


def moe_ffn(x, router_w, router_b, w_gu, b_gu, w_down, b_down):
    bsz, seq, d = x.shape
    xf = x.reshape(-1, d)
    logits = (xf @ router_w + router_b).astype(jnp.float32)
    top_v, top_i = lax.top_k(logits, TOP_K)
    gates = jax.nn.softmax(top_v, axis=-1)
    gate_full = jnp.sum(jax.nn.one_hot(top_i, N_EXPERTS, dtype=jnp.float32) * gates[..., None], axis=1)
    y = jnp.zeros((xf.shape[0], d), jnp.float32)
    for e in range(N_EXPERTS):
        gu = xf @ w_gu[e] + b_gu[e]
        g_, u_ = jnp.split(gu, 2, axis=-1)
        g_ = jnp.minimum(g_, SWIGLU_LIMIT)
        u_ = jnp.clip(u_, -SWIGLU_LIMIT, SWIGLU_LIMIT)
        act = (u_ + 1.0) * g_ * jax.nn.sigmoid(SWIGLU_ALPHA * g_)
        y = y + gate_full[:, e:e + 1] * (act @ w_down[e] + b_down[e])
    return y.astype(x.dtype).reshape(bsz, seq, d)


def setup_inputs(seed: int = 0) -> dict:
    key = jax.random.key(seed)
    ks = jax.random.split(key, 40)
    f32 = jnp.float32
    L = DEPTH

    def nrm(k, shape, scale):
        return jax.random.normal(k, shape, f32) * scale

    offset = jax.random.randint(ks[1], (BATCH, 1), 0, 1024) * CHUNK
    positions = (offset + jnp.arange(SEQ)[None, :]).astype(jnp.int32)
    dt0 = jnp.exp(jax.random.uniform(ks[9], (L, SSM_HEADS), f32, math.log(1e-3), math.log(1e-1)))
    return {
        'x': nrm(ks[0], (BATCH, SEQ, D_MODEL), 1.0),
        'positions': positions,
        'w_in': nrm(ks[2], (L, D_MODEL, IN_COLS), D_MODEL ** -0.5),
        'gm_ln_g': 1.0 + nrm(ks[3], (L, GM_HEADS, GM_DIM), 0.02),
        'gm_ln_b': nrm(ks[4], (L, GM_HEADS, GM_DIM), 0.02),
        'gm_ws': nrm(ks[5], (L, GM_HEADS, GM_BLOCK, GM_BLOCK), GM_BLOCK ** -0.5),
        'gm_bs': 1.0 + nrm(ks[6], (L, GM_HEADS, GM_BLOCK), 0.02),
        'ssm_conv_w': nrm(ks[7], (L, SSM_CONV, SSM_XBC), SSM_CONV ** -0.5),
        'ssm_conv_b': nrm(ks[8], (L, SSM_XBC), 0.02),
        'ssm_dt_bias': dt0 + jnp.log(-jnp.expm1(-dt0)),
        'ssm_a_log': jnp.log(jax.random.uniform(ks[10], (L, SSM_HEADS), f32, 1.0, 16.0)),
        'ssm_d': 1.0 + nrm(ks[11], (L, SSM_HEADS), 0.1),
        'ssm_norm_g': 1.0 + nrm(ks[12], (L, GROUP_WIDTH), 0.02),
        'ml_b_i': nrm(ks[13], (L, ML_HEADS), 0.1),
        'ml_b_f': jax.random.uniform(ks[14], (L, ML_HEADS), f32, 3.0, 6.0),
        'ml_norm_g': 1.0 + nrm(ks[15], (L, ML_HEADS, ML_DIM), 0.02),
        'att_q_norm_g': 1.0 + nrm(ks[16], (L, Q_RANK), 0.02),
        'att_w_uq': nrm(ks[17], (L, Q_RANK, GROUP_WIDTH + IDX_HEADS * IDX_DIM), Q_RANK ** -0.5),
        'idx_k_ln_g': 1.0 + nrm(ks[18], (L, IDX_DIM), 0.02),
        'idx_k_ln_b': nrm(ks[19], (L, IDX_DIM), 0.02),
        'w_out': nrm(ks[20], (L, MIX_WIDTH, D_MODEL), MIX_WIDTH ** -0.5 * DEEPNORM_BETA),
        'ln1_g': 1.0 + nrm(ks[21], (L, D_MODEL), 0.02),
        'ln1_b': nrm(ks[22], (L, D_MODEL), 0.02),
        'ln2_g': 1.0 + nrm(ks[23], (L, D_MODEL), 0.02),
        'ln2_b': nrm(ks[24], (L, D_MODEL), 0.02),
        'router_w': nrm(ks[25], (L, D_MODEL, N_EXPERTS), D_MODEL ** -0.5),
        'router_b': nrm(ks[26], (L, N_EXPERTS), 0.01),
        'expert_w_gu': nrm(ks[27], (L, N_EXPERTS, D_MODEL, 2 * D_FF), D_MODEL ** -0.5 * DEEPNORM_BETA),
        'expert_b_gu': nrm(ks[28], (L, N_EXPERTS, 2 * D_FF), 0.01),
        'expert_w_down': nrm(ks[29], (L, N_EXPERTS, D_FF, D_MODEL), D_FF ** -0.5 * DEEPNORM_BETA),
        'expert_b_down': nrm(ks[30], (L, N_EXPERTS, D_MODEL), 0.01),
    }


def reference(x, positions, w_in, gm_ln_g, gm_ln_b, gm_ws, gm_bs, ssm_conv_w, ssm_conv_b,
              ssm_dt_bias, ssm_a_log, ssm_d, ssm_norm_g, ml_b_i, ml_b_f, ml_norm_g,
              att_q_norm_g, att_w_uq, idx_k_ln_g, idx_k_ln_b, w_out, ln1_g, ln1_b, ln2_g, ln2_b,
              router_w, router_b, expert_w_gu, expert_b_gu, expert_w_down, expert_b_down):
    splits = [GM_COLS, GM_COLS + SSM_COLS, GM_COLS + SSM_COLS + ML_COLS]
    for l in range(DEPTH):
        h = jnp.einsum('bsd,de->bse', x, w_in[l])
        h_gm, h_ssm, h_ml, h_att = jnp.split(h, splits, axis=-1)
        y_gm = gmlp_mixer(h_gm, gm_ln_g[l], gm_ln_b[l], gm_ws[l], gm_bs[l])
        y_ssm = mamba2_mixer(h_ssm, ssm_conv_w[l], ssm_conv_b[l], ssm_dt_bias[l], ssm_a_log[l],
                             ssm_d[l], ssm_norm_g[l])
        y_ml = mlstm_mixer(h_ml, ml_b_i[l], ml_b_f[l], ml_norm_g[l])
        y_att = dsa_mixer(h_att, positions, att_q_norm_g[l], att_w_uq[l], idx_k_ln_g[l], idx_k_ln_b[l])
        mix = jnp.einsum('bse,ed->bsd', jnp.concatenate([y_gm, y_ssm, y_ml, y_att], axis=-1), w_out[l])
        x = layer_norm(DEEPNORM_ALPHA * x + mix, ln1_g[l], ln1_b[l])
        ffn = moe_ffn(x, router_w[l], router_b[l], expert_w_gu[l], expert_b_gu[l],
                      expert_w_down[l], expert_b_down[l])
        x = layer_norm(DEEPNORM_ALPHA * x + ffn, ln2_g[l], ln2_b[l])
    return x
```

```python
import functools
import math

import numpy as np
import jax
import jax.numpy as jnp
from jax import lax
from jax.experimental import pallas as pl
from jax.experimental.pallas import tpu as pltpu

F32 = jnp.float32
BF16 = jnp.bfloat16
I32 = jnp.int32

LANES = 128
CHUNK = 64
GROUP_WIDTH = 512
GM_BLOCK = 128
GM_HEADS = 4
SSM_HEADS = 8
SSM_HEADDIM = 64
SSM_STATE = 128
SSM_CONV = 4
ML_HEADS = 4
ML_DIM = 128
ATT_HEADS = 4
ATT_DIM = 128
Q_RANK = 384
IDX_HEADS = 8
IDX_DIM = 64
IDX_TOPK_MAX = 256
ROPE_THETA = 500000.0
N_EXPERTS = 32
TOP_K = 4
SWIGLU_LIMIT = 7.0
SWIGLU_ALPHA = 1.702

SEQ_CHUNK = 128
SEQ_TILE = 512
DSA_QB = 256
DSA_TK = 512
MOE_TM = 512
VMEM_LIMIT = 56 * 1024 * 1024

NEG_BIG = -1e30
INT_MIN = -(2 ** 31)
INT_MAX = 2 ** 31 - 1
NEG_INF_KEY = int(np.int32(np.uint32(0xFF800000) ^ np.uint32(0x7FFFFFFF)))

H_GM, H_XBC, H_Z, H_Q, H_K, H_V, H_O, H_AK, H_AV, H_CQ, H_COLS = (
    0, 1024, 2048, 2560, 3072, 3584, 4096, 4608, 5120, 5632, 6144)
HS_COLS = 384


def _cparams(*sem):
    return pltpu.CompilerParams(dimension_semantics=sem, vmem_limit_bytes=VMEM_LIMIT)


def _dot(a, b):
    return jnp.dot(a, b, preferred_element_type=F32)


def _dot_nt(a, b):
    return lax.dot_general(a, b, (((1,), (1,)), ((), ())), preferred_element_type=F32)


def _dot_exact_lhs(tri, x):
    hi = x.astype(BF16)
    r1 = x - hi.astype(F32)
    mid = r1.astype(BF16)
    lo = (r1 - mid.astype(F32)).astype(BF16)
    return _dot(tri, hi) + _dot(tri, mid) + _dot(tri, lo)


def _lower_tri(n):
    r = lax.broadcasted_iota(I32, (n, n), 0)
    c = lax.broadcasted_iota(I32, (n, n), 1)
    return r >= c


def _sigmoid(x):
    return 1.0 / (1.0 + jnp.exp(-x))


def _softplus(x):
    return jnp.maximum(x, 0.0) + jnp.log(1.0 + jnp.exp(-jnp.abs(x)))


def _log_sigmoid(x):
    return -_softplus(-x)


def _mm_kernel(a_ref, b_ref, o_ref):
    o_ref[...] = _dot(a_ref[...], b_ref[...]).astype(o_ref.dtype)


def _matmul(a, b, out_dtype, tm, tn):
    m, k = a.shape
    n = b.shape[1]
    return pl.pallas_call(
        _mm_kernel,
        grid=(n // tn, m // tm),
        in_specs=[pl.BlockSpec((tm, k), lambda j, i: (i, 0)),
                  pl.BlockSpec((k, tn), lambda j, i: (0, j))],
        out_specs=pl.BlockSpec((tm, tn), lambda j, i: (i, j)),
        out_shape=jax.ShapeDtypeStruct((m, n), out_dtype),
        compiler_params=_cparams("parallel", "parallel"),
        name="in_proj",
    )(a, b)


def _gmlp_kernel(h_ref, lng_ref, lnb_ref, ws_ref, bs_ref, o_ref):
    t = h_ref.shape[0]
    h = h_ref[...].astype(F32)
    h = 0.5 * h * (1.0 + lax.erf(h * (1.0 / math.sqrt(2.0))))
    r = lax.broadcasted_iota(I32, (GM_BLOCK, GM_BLOCK), 0)
    c = lax.broadcasted_iota(I32, (GM_BLOCK, GM_BLOCK), 1)
    allowed = (c // CHUNK) <= (r // CHUNK)
    for g in range(GM_HEADS):
        u = h[:, g * LANES:(g + 1) * LANES]
        v = h[:, GROUP_WIDTH + g * LANES:GROUP_WIDTH + (g + 1) * LANES]
        mu = jnp.mean(v, axis=-1, keepdims=True)
        vc = v - mu
        var = jnp.mean(vc * vc, axis=-1, keepdims=True)
        vn = vc * lax.rsqrt(var + 1e-5) * lng_ref[:, g * LANES:(g + 1) * LANES] \
            + lnb_ref[:, g * LANES:(g + 1) * LANES]
        w = jnp.where(allowed, ws_ref[g], 0.0).astype(BF16)
        for wdw in range(t // GM_BLOCK):
            rows = slice(wdw * GM_BLOCK, (wdw + 1) * GM_BLOCK)
            vmix = _dot(w, vn[rows].astype(BF16)) + bs_ref[g]
            o_ref[rows, g * LANES:(g + 1) * LANES] = (u[rows] * vmix).astype(o_ref.dtype)


def _gmlp(h, lng, lnb, ws, bs, tile=256):
    n = h.shape[0]
    return pl.pallas_call(
        _gmlp_kernel,
        grid=(n // tile,),
        in_specs=[pl.BlockSpec((tile, 2 * GROUP_WIDTH), lambda i: (i, H_GM // (2 * GROUP_WIDTH))),
                  pl.BlockSpec((1, GROUP_WIDTH), lambda i: (0, 0)),
                  pl.BlockSpec((1, GROUP_WIDTH), lambda i: (0, 0)),
                  pl.BlockSpec((GM_HEADS, GM_BLOCK, GM_BLOCK), lambda i: (0, 0, 0)),
                  pl.BlockSpec((GM_HEADS, GM_BLOCK, 1), lambda i: (0, 0, 0))],
        out_specs=pl.BlockSpec((tile, GROUP_WIDTH), lambda i: (i, 0)),
        out_shape=jax.ShapeDtypeStruct((n, GROUP_WIDTH), BF16),
        compiler_params=_cparams("parallel"),
        name="gmlp",
    )(h, lng, lnb, ws, bs)


def _ssd_kernel(xbc_ref, z_ref, dt_ref, cw_ref, cb_ref, dtb_ref, a_ref, dsk_ref, ng_ref, o_ref,
                buf_ref, act_ref, st_ref):
    t = xbc_ref.shape[0]
    lc = SEQ_CHUNK

    @pl.when(pl.program_id(1) == 0)
    def _():
        buf_ref[0:8, :] = jnp.zeros((8, buf_ref.shape[1]), F32)
        st_ref[...] = jnp.zeros(st_ref.shape, F32)

    buf_ref[8:8 + t, :] = xbc_ref[...].astype(F32)
    conv = cb_ref[...] + cw_ref[0:1, :] * buf_ref[5:5 + t, :]
    for k in range(1, SSM_CONV):
        conv = conv + cw_ref[k:k + 1, :] * buf_ref[5 + k:5 + k + t, :]
    act_ref[...] = conv * _sigmoid(conv)
    buf_ref[0:8, :] = buf_ref[t:t + 8, :]

    tri = _lower_tri(lc)
    tri_b = tri.astype(BF16)
    lane = lax.broadcasted_iota(I32, (lc, LANES), 1)
    first_half = lane < SSM_HEADDIM
    a_row = a_ref[...]

    def chunk(ci, carry):
        r0 = pl.multiple_of(ci * lc, lc)
        xa = act_ref[pl.ds(r0, lc), :]
        dt = _softplus(dt_ref[pl.ds(r0, lc), :] + dtb_ref[...])
        acs = _dot_exact_lhs(tri_b, dt * a_row)
        acs_t = acs.T
        ys = []
        for p in range(SSM_HEADS // 2):
            g = p // 2
            h0, h1 = 2 * p, 2 * p + 1
            bm = xa[:, GROUP_WIDTH + g * SSM_STATE:GROUP_WIDTH + (g + 1) * SSM_STATE]
            cm = xa[:, GROUP_WIDTH + 2 * SSM_STATE + g * SSM_STATE:GROUP_WIDTH + 2 * SSM_STATE + (g + 1) * SSM_STATE]
            cm_b = cm.astype(BF16)
            cb = _dot_nt(cm_b, bm.astype(BF16))
            x2 = xa[:, p * LANES:(p + 1) * LANES]
            dt2 = jnp.where(first_half, dt[:, h0:h0 + 1], dt[:, h1:h1 + 1])
            xdt = x2 * dt2
            xdt_b = xdt.astype(BF16)
            yd = []
            for hh in (h0, h1):
                decay = jnp.where(tri, jnp.exp(acs[:, hh:hh + 1] - acs_t[hh:hh + 1, :]), 0.0)
                yd.append(_dot((cb * decay).astype(BF16), xdt_b))
            y_diag = jnp.where(first_half, yd[0], yd[1])
            acs2 = jnp.where(first_half, acs[:, h0:h0 + 1], acs[:, h1:h1 + 1])
            aend2 = jnp.where(first_half[0:1], acs[lc - 1:lc, h0:h0 + 1], acs[lc - 1:lc, h1:h1 + 1])
            st = st_ref[p]
            y_off = _dot(cm_b, st.astype(BF16)) * jnp.exp(acs2)
            ys.append(y_diag + y_off + x2 * dsk_ref[:, p * LANES:(p + 1) * LANES])
            upd = _dot(bm.T.astype(BF16), (xdt * jnp.exp(aend2 - acs2)).astype(BF16))
            st_ref[p] = jnp.exp(aend2) * st + upd
        y = jnp.concatenate(ys, axis=-1)
        zz = z_ref[pl.ds(r0, lc), :].astype(F32)
        y = y * (zz * _sigmoid(zz))
        y = y * lax.rsqrt(jnp.mean(y * y, axis=-1, keepdims=True) + 1e-6) * ng_ref[...]
        o_ref[pl.ds(r0, lc), :] = y.astype(o_ref.dtype)
        return carry

    lax.fori_loop(0, t // lc, chunk, 0)


def _ssd(h, hs, bsz, seq, cw, cb, dtb, a_row, dsk, ng, tile=SEQ_TILE):
    n = h.shape[0]
    nt = seq // tile
    xw = 2 * GROUP_WIDTH
    return pl.pallas_call(
        _ssd_kernel,
        grid=(bsz, nt),
        in_specs=[pl.BlockSpec((tile, xw), lambda b, j: (b * nt + j, H_XBC // xw)),
                  pl.BlockSpec((tile, GROUP_WIDTH), lambda b, j: (b * nt + j, H_Z // GROUP_WIDTH)),
                  pl.BlockSpec((tile, LANES), lambda b, j: (b * nt + j, 1)),
                  pl.BlockSpec((SSM_CONV, xw), lambda b, j: (0, 0)),
                  pl.BlockSpec((1, xw), lambda b, j: (0, 0)),
                  pl.BlockSpec((1, LANES), lambda b, j: (0, 0)),
                  pl.BlockSpec((1, LANES), lambda b, j: (0, 0)),
                  pl.BlockSpec((1, GROUP_WIDTH), lambda b, j: (0, 0)),
                  pl.BlockSpec((1, GROUP_WIDTH), lambda b, j: (0, 0))],
        out_specs=pl.BlockSpec((tile, GROUP_WIDTH), lambda b, j: (b * nt + j, 0)),
        out_shape=jax.ShapeDtypeStruct((n, GROUP_WIDTH), BF16),
        scratch_shapes=[pltpu.VMEM((tile + 8, xw), F32),
                        pltpu.VMEM((tile, xw), F32),
                        pltpu.VMEM((SSM_HEADS // 2, SSM_STATE, LANES), F32)],
        compiler_params=_cparams("arbitrary", "arbitrary"),
        name="ssd",
    )(h, h, hs, cw, cb, dtb, a_row, dsk, ng)


def _mlstm_kernel(q_ref, k_ref, v_ref, og_ref, gt_ref, bias_ref, ng_ref, o_ref, c_ref, n_ref, m_ref):
    t = q_ref.shape[0]
    lc = SEQ_CHUNK
    scale = ML_DIM ** -0.5

    @pl.when(pl.program_id(1) == 0)
    def _():
        c_ref[...] = jnp.zeros(c_ref.shape, F32)
        n_ref[...] = jnp.zeros(n_ref.shape, F32)
        m_ref[...] = jnp.zeros(m_ref.shape, F32)

    tri = _lower_tri(lc)
    tri_b = tri.astype(BF16)

    def chunk(ci, carry):
        r0 = pl.multiple_of(ci * lc, lc)
        gts = gt_ref[pl.ds(r0, lc), :] + bias_ref[...]
        bcum = _dot_exact_lhs(tri_b, _log_sigmoid(gts))
        gts_t = gts.T
        bcum_t = bcum.T
        for hd in range(ML_HEADS):
            cols = slice(hd * ML_DIM, (hd + 1) * ML_DIM)
            qh = q_ref[pl.ds(r0, lc), cols]
            kh = (k_ref[pl.ds(r0, lc), cols].astype(F32) * scale).astype(BF16)
            vh = v_ref[pl.ds(r0, lc), cols]
            i_col = gts[:, hd:hd + 1]
            i_row = gts_t[hd:hd + 1, :]
            b_col = bcum[:, ML_HEADS + hd:ML_HEADS + hd + 1]
            b_row = bcum_t[ML_HEADS + hd:ML_HEADS + hd + 1, :]
            b_end = b_col[lc - 1:lc, :]
            m_prev = m_ref[hd][0:1, 0:1]
            c_prev = c_ref[hd]
            n_prev = n_ref[hd][0:1, :]
            log_d = jnp.where(tri, b_col - b_row + i_row, -jnp.inf)
            inter = b_col + m_prev
            m_j = jnp.maximum(jnp.max(log_d, axis=-1, keepdims=True), inter)
            s_mat = jnp.exp(log_d - m_j) * _dot_nt(qh, kh)
            gsc = jnp.exp(inter - m_j)
            num = _dot(s_mat.astype(BF16), vh) + gsc * _dot_nt(qh, c_prev.astype(BF16))
            den = jnp.sum(s_mat, axis=-1, keepdims=True) \
                + gsc * jnp.sum(qh.astype(F32) * n_prev, axis=-1, keepdims=True)
            hh = num / jnp.maximum(jnp.abs(den), jnp.exp(-m_j))
            mu = jnp.mean(hh, axis=-1, keepdims=True)
            hc = hh - mu
            var = jnp.mean(hc * hc, axis=-1, keepdims=True)
            hn = hc * lax.rsqrt(var + 1e-5) * ng_ref[:, cols]
            og = og_ref[pl.ds(r0, lc), cols].astype(F32)
            o_ref[pl.ds(r0, lc), cols] = (hn * _sigmoid(og)).astype(o_ref.dtype)
            a_col = b_end - b_col + i_col
            a_row = b_end - b_row + i_row
            m_new = jnp.maximum(b_end + m_prev, jnp.max(a_row, axis=-1, keepdims=True))
            decay = jnp.exp(b_end + m_prev - m_new)
            w_col = jnp.exp(a_col - m_new)
            kf = kh.astype(F32)
            vw_t = (vh.astype(F32) * w_col).T.astype(BF16)
            c_ref[hd] = decay * c_prev + _dot(vw_t, kh)
            n_new = decay * n_prev + jnp.sum(w_col * kf, axis=0, keepdims=True)
            n_ref[hd] = jnp.broadcast_to(n_new, (8, ML_DIM))
            m_ref[hd] = jnp.broadcast_to(m_new, (8, LANES))
        return carry

    lax.fori_loop(0, t // lc, chunk, 0)


def _mlstm(h, hs, bsz, seq, bias, ng, tile=SEQ_TILE):
    n = h.shape[0]
    nt = seq // tile
    gw = GROUP_WIDTH

    def col(c0):
        return pl.BlockSpec((tile, gw), lambda b, j: (b * nt + j, c0 // gw))

    return pl.pallas_call(
        _mlstm_kernel,
        grid=(bsz, nt),
        in_specs=[col(H_Q), col(H_K), col(H_V), col(H_O),
                  pl.BlockSpec((tile, LANES), lambda b, j: (b * nt + j, 2)),
                  pl.BlockSpec((1, LANES), lambda b, j: (0, 0)),
                  pl.BlockSpec((1, gw), lambda b, j: (0, 0))],
        out_specs=pl.BlockSpec((tile, gw), lambda b, j: (b * nt + j, 0)),
        out_shape=jax.ShapeDtypeStruct((n, gw), BF16),
        scratch_shapes=[pltpu.VMEM((ML_HEADS, ML_DIM, ML_DIM), F32),
                        pltpu.VMEM((ML_HEADS, 8, ML_DIM), F32),
                        pltpu.VMEM((ML_HEADS, 8, LANES), F32)],
        compiler_params=_cparams("arbitrary", "arbitrary"),
        name="mlstm",
    )(h, h, h, h, hs, bias, ng)


def _att_prep_kernel(cq_ref, ak_ref, hs_ref, pos_ref, qg_ref, wuq_ref, kg_ref, kb_ref,
                     fm_ref, sm_ref, fi_ref, si_ref, fk_ref, sk_ref,
                     q_ref, k_ref, qi_ref, ki_ref, w_ref):
    t = cq_ref.shape[0]
    lane = lax.broadcasted_iota(I32, (t, LANES), 1)
    pos = pos_ref[...].astype(F32)

    cq = cq_ref[:, 0:Q_RANK].astype(F32)
    cq = cq * lax.rsqrt(jnp.mean(cq * cq, axis=-1, keepdims=True) + 1e-6) * qg_ref[...]
    q_all = _dot(cq.astype(BF16), wuq_ref[...])

    def rope(x, cos, sin_signed, half, period):
        first = (lane % period) < half
        partner = jnp.where(first, pltpu.roll(x, LANES - half, 1), pltpu.roll(x, half, 1))
        return x * cos + partner * sin_signed

    ang = pos * fm_ref[...]
    cos_m, sin_m = jnp.cos(ang), jnp.sin(ang) * sm_ref[...]
    half_m = ATT_DIM // 8
    for hd in range(ATT_HEADS):
        cols = slice(hd * ATT_DIM, (hd + 1) * ATT_DIM)
        q_ref[:, cols] = (rope(q_all[:, cols], cos_m, sin_m, half_m, LANES) * (ATT_DIM ** -0.5)).astype(q_ref.dtype)
        k_ref[:, cols] = rope(ak_ref[:, cols].astype(F32), cos_m, sin_m, half_m, LANES).astype(k_ref.dtype)

    ang = pos * fi_ref[...]
    cos_i, sin_i = jnp.cos(ang), jnp.sin(ang) * si_ref[...]
    half_i = IDX_DIM // 8
    low = lane < IDX_DIM
    for pr in range(IDX_HEADS // 2):
        x = rope(q_all[:, GROUP_WIDTH + pr * LANES:GROUP_WIDTH + (pr + 1) * LANES], cos_i, sin_i, half_i, IDX_DIM)
        qi_ref[:, (2 * pr) * LANES:(2 * pr + 1) * LANES] = jnp.where(low, x, 0.0).astype(qi_ref.dtype)
        qi_ref[:, (2 * pr + 1) * LANES:(2 * pr + 2) * LANES] = \
            jnp.where(low, pltpu.roll(x, IDX_DIM, 1), 0.0).astype(qi_ref.dtype)

    hs = hs_ref[...]
    kx = jnp.where(low, hs, 0.0)
    mu = jnp.sum(kx, axis=-1, keepdims=True) * (1.0 / IDX_DIM)
    kc = jnp.where(low, hs - mu, 0.0)
    var = jnp.sum(kc * kc, axis=-1, keepdims=True) * (1.0 / IDX_DIM)
    kn = kc * lax.rsqrt(var + 1e-5) * kg_ref[...] + kb_ref[...]
    ang = pos * fk_ref[...]
    ki_ref[...] = rope(kn, jnp.cos(ang), jnp.sin(ang) * sk_ref[...], half_i, LANES).astype(ki_ref.dtype)
    w_ref[...] = jnp.where(lane < IDX_HEADS, pltpu.roll(hs, LANES - IDX_DIM, 1), 0.0) \
        * (IDX_HEADS ** -0.5 * IDX_DIM ** -0.5)


def _att_prep(h, hs, pos, qg, wuq, kg, kb, rope_rows, tile=256):
    n = h.shape[0]
    gw = GROUP_WIDTH
    row = lambda w: pl.BlockSpec((1, w), lambda i: (0, 0))
    return pl.pallas_call(
        _att_prep_kernel,
        grid=(n // tile,),
        in_specs=[pl.BlockSpec((tile, gw), lambda i: (i, H_CQ // gw)),
                  pl.BlockSpec((tile, gw), lambda i: (i, H_AK // gw)),
                  pl.BlockSpec((tile, LANES), lambda i: (i, 0)),
                  pl.BlockSpec((tile, 1), lambda i: (i, 0)),
                  row(Q_RANK),
                  pl.BlockSpec(wuq.shape, lambda i: (0, 0)),
                  row(LANES), row(LANES)] + [row(LANES)] * 6,
        out_specs=[pl.BlockSpec((tile, gw), lambda i: (i, 0)),
                   pl.BlockSpec((tile, gw), lambda i: (i, 0)),
                   pl.BlockSpec((tile, IDX_HEADS * LANES), lambda i: (i, 0)),
                   pl.BlockSpec((tile, LANES), lambda i: (i, 0)),
                   pl.BlockSpec((tile, LANES), lambda i: (i, 0))],
        out_shape=[jax.ShapeDtypeStruct((n, gw), BF16),
                   jax.ShapeDtypeStruct((n, gw), BF16),
                   jax.ShapeDtypeStruct((n, IDX_HEADS * LANES), BF16),
                   jax.ShapeDtypeStruct((n, LANES), BF16),
                   jax.ShapeDtypeStruct((n, LANES), F32)],
        compiler_params=_cparams("parallel"),
        name="att_prep",
    )(h, h, hs, pos, qg, wuq, kg, kb, *rope_rows)


def _index_keys(qi, w, kk, sidx, lim):
    acc = None
    for hd in range(IDX_HEADS):
        rel = jnp.maximum(_dot_nt(qi[:, hd * LANES:(hd + 1) * LANES], kk), 0.0)
        term = rel * w[:, hd:hd + 1]
        acc = term if acc is None else acc + term
    score = jnp.where(sidx < lim, acc + 0.0, -jnp.inf)
    bits = pltpu.bitcast(score, I32)
    return bits ^ ((bits >> 31) & INT_MAX)


def _dsa_thr_kernel(qi_ref, w_ref, ki_ref, thr_ref, jst_ref, key_ref, *, top_k, idx_bits):
    qb = qi_ref.shape[0]
    tk = key_ref.shape[2]
    i = pl.program_id(1)
    n_tiles = (i * qb + qb + tk - 1) // tk
    row = lax.broadcasted_iota(I32, (qb, 1), 0)
    lim = ((i * qb + row) // CHUNK + 1) * CHUNK
    lane_t = lax.broadcasted_iota(I32, (qb, tk), 1)
    qi = qi_ref[...]
    w = w_ref[...]

    def fill(kt, carry):
        k0 = pl.multiple_of(kt * tk, tk)
        key_ref[kt] = _index_keys(qi, w, ki_ref[pl.ds(k0, tk), :], lane_t + k0, lim)
        return carry

    lax.fori_loop(0, n_tiles, fill, 0)

    def count(pred):
        def body(kt, acc):
            m = pred(key_ref[kt], kt).astype(I32)
            for c in range(tk // LANES):
                acc = acc + m[:, c * LANES:(c + 1) * LANES]
            return acc
        acc = lax.fori_loop(0, n_tiles, body, jnp.zeros((qb, LANES), I32))
        return jnp.sum(acc, axis=-1, keepdims=True)

    def count_ge(cand):
        return count(lambda keys, kt: keys >= cand)

    thr = jnp.where(count_ge(jnp.zeros((qb, 1), I32)) >= top_k, 0, INT_MIN).astype(I32)

    def bit_step(it, thr):
        cand = thr | lax.shift_left(jnp.int32(1), 30 - it)
        return jnp.where(count_ge(cand) >= top_k, cand, thr)

    thr = lax.fori_loop(0, 31, bit_step, thr)
    thr_ref[...] = jnp.broadcast_to(thr, thr_ref.shape)
    jst_ref[...] = jnp.full(jst_ref.shape, INT_MAX, I32)

    excess = (count_ge(thr) > top_k) & (thr > NEG_INF_KEY)

    @pl.when(jnp.max(excess.astype(I32)) > 0)
    def _():
        need = top_k - count(lambda keys, kt: keys > thr)

        def idx_step(it, bound):
            cand = bound | lax.shift_left(jnp.int32(1), idx_bits - 1 - it)
            below = count(lambda keys, kt: (keys == thr) & (lane_t + kt * tk < cand))
            return jnp.where(below <= need, cand, bound)

        bound = lax.fori_loop(0, idx_bits, idx_step, jnp.zeros((qb, 1), I32))
        jst_ref[...] = jnp.broadcast_to(jnp.where(excess, bound, INT_MAX), jst_ref.shape)


def _dsa_thresholds(qi, w, ki, bsz, seq, top_k):
    n = qi.shape[0]
    qb, tk = DSA_QB, DSA_TK
    nq = seq // qb
    kern = functools.partial(_dsa_thr_kernel, top_k=top_k, idx_bits=int(seq).bit_length())
    return pl.pallas_call(
        kern,
        grid=(bsz, nq),
        in_specs=[pl.BlockSpec((qb, IDX_HEADS * LANES), lambda b, i: (b * nq + i, 0)),
                  pl.BlockSpec((qb, LANES), lambda b, i: (b * nq + i, 0)),
                  pl.BlockSpec((seq, LANES), lambda b, i: (b, 0))],
        out_specs=[pl.BlockSpec((qb, LANES), lambda b, i: (b * nq + i, 0)),
                   pl.BlockSpec((qb, LANES), lambda b, i: (b * nq + i, 0))],
        out_shape=[jax.ShapeDtypeStruct((n, LANES), I32), jax.ShapeDtypeStruct((n, LANES), I32)],
        scratch_shapes=[pltpu.VMEM((seq // tk, qb, tk), I32)],
        compiler_params=_cparams("parallel", "parallel"),
        name="dsa_thresholds",
    )(qi, w, ki)


def _dsa_attn_kernel(qt_ref, kt_ref, first_ref, last_ref,
                     q_ref, qi_ref, w_ref, thr_ref, jst_ref, k_ref, v_ref, ki_ref, o_ref,
                     m_ref, l_ref, acc_ref):
    s = pl.program_id(1)
    qb, tk = q_ref.shape[0], k_ref.shape[0]
    qblk, kt = qt_ref[s], kt_ref[s]

    @pl.when(first_ref[s] == 1)
    def _():
        m_ref[...] = jnp.full(m_ref.shape, NEG_BIG, F32)
        l_ref[...] = jnp.zeros(l_ref.shape, F32)
        acc_ref[...] = jnp.zeros(acc_ref.shape, F32)

    row = lax.broadcasted_iota(I32, (qb, 1), 0)
    lim = ((qblk * qb + row) // CHUNK + 1) * CHUNK
    sidx = lax.broadcasted_iota(I32, (qb, tk), 1) + kt * tk
    keys = _index_keys(qi_ref[...], w_ref[...], ki_ref[...], sidx, lim)
    thr = thr_ref[:, 0:1]
    sel = ((keys > thr) | ((keys == thr) & (sidx < jst_ref[:, 0:1]))) & (sidx < lim)

    for hd in range(ATT_HEADS):
        cols = slice(hd * ATT_DIM, (hd + 1) * ATT_DIM)
        logits = jnp.where(sel, _dot_nt(q_ref[:, cols], k_ref[:, cols]), -jnp.inf)
        m_prev = m_ref[:, cols][:, 0:1]
        m_new = jnp.maximum(m_prev, jnp.max(logits, axis=-1, keepdims=True))
        p = jnp.exp(logits - m_new)
        alpha = jnp.exp(m_prev - m_new)
        l_ref[:, cols] = alpha * l_ref[:, cols] + jnp.sum(p, axis=-1, keepdims=True)
        acc_ref[:, cols] = alpha * acc_ref[:, cols] + _dot(p.astype(BF16), v_ref[:, cols])
        m_ref[:, cols] = jnp.broadcast_to(m_new, (qb, ATT_DIM))

    @pl.when(last_ref[s] == 1)
    def _():
        o_ref[...] = (acc_ref[...] / l_ref[...]).astype(o_ref.dtype)


def _dsa_attention(q, qi, w, thr, jst, k, h, ki, bsz, seq):
    n = q.shape[0]
    qb, tk = DSA_QB, DSA_TK
    nq, nkt = seq // qb, seq // tk
    pairs = [(i, t) for i in range(nq) for t in range((i * qb + qb - 1) // tk + 1)]
    qt = jnp.asarray([p[0] for p in pairs], I32)
    ktab = jnp.asarray([p[1] for p in pairs], I32)
    first = jnp.asarray([1 if p[1] == 0 else 0 for p in pairs], I32)
    last = jnp.asarray([1 if p[1] == (p[0] * qb + qb - 1) // tk else 0 for p in pairs], I32)
    gw = GROUP_WIDTH

    def qmap(w_):
        return pl.BlockSpec((qb, w_), lambda b, s, qt, kt, f, l: (b * nq + qt[s], 0))

    def kmap(w_, cb):
        return pl.BlockSpec((tk, w_), lambda b, s, qt, kt, f, l: (b * nkt + kt[s], cb))

    grid_spec = pltpu.PrefetchScalarGridSpec(
        num_scalar_prefetch=4,
        grid=(bsz, len(pairs)),
        in_specs=[qmap(gw), qmap(IDX_HEADS * LANES), qmap(LANES), qmap(LANES), qmap(LANES),
                  kmap(gw, 0), kmap(gw, H_AV // gw), kmap(LANES, 0)],
        out_specs=qmap(gw),
        scratch_shapes=[pltpu.VMEM((qb, gw), F32), pltpu.VMEM((qb, gw), F32), pltpu.VMEM((qb, gw), F32)],
    )
    return pl.pallas_call(
        _dsa_attn_kernel,
        grid_spec=grid_spec,
        out_shape=jax.ShapeDtypeStruct((n, gw), BF16),
        compiler_params=_cparams("arbitrary", "arbitrary"),
        name="dsa_attention",
    )(qt, ktab, first, last, q, qi, w, thr, jst, k, h, ki)


def _mix_kernel(ya_ref, yb_ref, yc_ref, yd_ref, x_ref, wo_ref, g_ref, b_ref, rw_ref, rb_ref,
                x1_ref, ti_ref, gt_ref, *, alpha):
    gw = GROUP_WIDTH
    mix = _dot(ya_ref[...], wo_ref[0:gw, :])
    mix = mix + _dot(yb_ref[...], wo_ref[gw:2 * gw, :])
    mix = mix + _dot(yc_ref[...], wo_ref[2 * gw:3 * gw, :])
    mix = mix + _dot(yd_ref[...], wo_ref[3 * gw:4 * gw, :])
    y = alpha * x_ref[...] + mix
    mu = jnp.mean(y, axis=-1, keepdims=True)
    yc = y - mu
    var = jnp.mean(yc * yc, axis=-1, keepdims=True)
    x1 = yc * lax.rsqrt(var + 1e-5) * g_ref[...] + b_ref[...]
    x1_ref[...] = x1

    logits = jnp.dot(x1, rw_ref[...], precision=lax.Precision.HIGHEST, preferred_element_type=F32) + rb_ref[...]
    lane = lax.broadcasted_iota(I32, logits.shape, 1)
    vals, idxs = [], []
    for _ in range(TOP_K):
        mx = jnp.max(logits, axis=-1, keepdims=True)
        ix = jnp.min(jnp.where(logits == mx, lane, LANES), axis=-1, keepdims=True)
        vals.append(mx)
        idxs.append(ix)
        logits = jnp.where(lane == ix, -jnp.inf, logits)
    es = [jnp.exp(v - vals[0]) for v in vals]
    tot = es[0] + es[1] + es[2] + es[3]
    ti = jnp.zeros(logits.shape, I32)
    gt = jnp.zeros(logits.shape, F32)
    for k in range(TOP_K):
        ti = jnp.where(lane == k, idxs[k], ti)
        gt = jnp.where(lane == k, es[k] / tot, gt)
    ti_ref[...] = ti
    gt_ref[...] = gt


def _mix_ln_router(ys, x, wo, g, b, rw, rb, alpha, tile=256):
    n, d = x.shape
    gw = GROUP_WIDTH
    yspec = pl.BlockSpec((tile, gw), lambda i: (i, 0))
    row = lambda w: pl.BlockSpec((1, w), lambda i: (0, 0))
    return pl.pallas_call(
        functools.partial(_mix_kernel, alpha=alpha),
        grid=(n // tile,),
        in_specs=[yspec, yspec, yspec, yspec,
                  pl.BlockSpec((tile, d), lambda i: (i, 0)),
                  pl.BlockSpec(wo.shape, lambda i: (0, 0)),
                  row(d), row(d),
                  pl.BlockSpec(rw.shape, lambda i: (0, 0)),
                  row(LANES)],
        out_specs=[pl.BlockSpec((tile, d), lambda i: (i, 0)),
                   pl.BlockSpec((tile, LANES), lambda i: (i, 0)),
                   pl.BlockSpec((tile, LANES), lambda i: (i, 0))],
        out_shape=[jax.ShapeDtypeStruct((n, d), F32),
                   jax.ShapeDtypeStruct((n, LANES), I32),
                   jax.ShapeDtypeStruct((n, LANES), F32)],
        compiler_params=_cparams("parallel"),
        name="mix_ln_router",
    )(*ys, x, wo, g, b, rw, rb)


def _row_copy(src, s, dst, d, sem):
    return pltpu.make_async_copy(src.at[pl.ds(s, 1), :], dst.at[pl.ds(d, 1), :], sem)


def _dispatch_kernel(pos_ref, x_ref, xs_in_ref, xs_ref, sem):
    del xs_in_ref
    tc = x_ref.shape[0]

    def issue(r, carry):
        for k in range(TOP_K):
            _row_copy(x_ref, r, xs_ref, pos_ref[0, 0, r * TOP_K + k], sem).start()
        return carry

    lax.fori_loop(0, tc, issue, 0)

    def drain(r, carry):
        for k in range(TOP_K):
            _row_copy(x_ref, 0, xs_ref, 0, sem).wait()
        return carry

    lax.fori_loop(0, tc, drain, 0)


def _dispatch(pos, x1, n_slots, tile=256):
    n, d = x1.shape
    pos3 = pos.reshape(n // tile, 1, tile * TOP_K)
    xs0 = jnp.zeros((n_slots, d), F32)
    return pl.pallas_call(
        _dispatch_kernel,
        grid=(n // tile,),
        in_specs=[pl.BlockSpec((1, 1, tile * TOP_K), lambda i: (i, 0, 0), memory_space=pltpu.SMEM),
                  pl.BlockSpec((tile, d), lambda i: (i, 0)),
                  pl.BlockSpec(memory_space=pl.ANY)],
        out_specs=pl.BlockSpec(memory_space=pl.ANY),
        out_shape=jax.ShapeDtypeStruct((n_slots, d), F32),
        scratch_shapes=[pltpu.SemaphoreType.DMA(())],
        input_output_aliases={2: 0},
        compiler_params=_cparams("arbitrary"),
        name="moe_dispatch",
    )(pos3, x1, xs0)


def _expert_up_kernel(te_ref, nu_ref, x_ref, wg_ref, wu_ref, bg_ref, bu_ref, o_ref):
    @pl.when(pl.program_id(1) < nu_ref[0])
    def _():
        xb = x_ref[...].astype(BF16)
        g = _dot(xb, wg_ref[...]) + bg_ref[...]
        u = _dot(xb, wu_ref[...]) + bu_ref[...]
        g = jnp.minimum(g, SWIGLU_LIMIT)
        u = jnp.clip(u, -SWIGLU_LIMIT, SWIGLU_LIMIT)
        o_ref[...] = ((u + 1.0) * g * _sigmoid(SWIGLU_ALPHA * g)).astype(o_ref.dtype)


def _expert_up(te, nu, xs, w_gu, b_gu, splits=2):
    p, d = xs.shape
    ff = w_gu.shape[2] // 2
    fj = ff // splits
    tm = MOE_TM
    tile_of = lambda i, nu: jnp.minimum(i, nu[0] - 1)
    grid_spec = pltpu.PrefetchScalarGridSpec(
        num_scalar_prefetch=2,
        grid=(splits, p // tm),
        in_specs=[pl.BlockSpec((tm, d), lambda j, i, te, nu: (tile_of(i, nu), 0)),
                  pl.BlockSpec((None, d, fj), lambda j, i, te, nu: (te[i], 0, j)),
                  pl.BlockSpec((None, d, fj), lambda j, i, te, nu: (te[i], 0, splits + j)),
                  pl.BlockSpec((None, 1, fj), lambda j, i, te, nu: (te[i], 0, j)),
                  pl.BlockSpec((None, 1, fj), lambda j, i, te, nu: (te[i], 0, splits + j))],
        out_specs=pl.BlockSpec((tm, fj), lambda j, i, te, nu: (tile_of(i, nu), j)),
    )
    return pl.pallas_call(
        _expert_up_kernel,
        grid_spec=grid_spec,
        out_shape=jax.ShapeDtypeStruct((p, ff), BF16),
        compiler_params=_cparams("arbitrary", "arbitrary"),
        name="expert_up",
    )(te, nu, xs, w_gu, w_gu, b_gu, b_gu)


def _expert_down_kernel(te_ref, nu_ref, a_ref, wd_ref, bd_ref, o_ref):
    @pl.when(pl.program_id(0) < nu_ref[0])
    def _():
        o_ref[...] = _dot(a_ref[...], wd_ref[...]) + bd_ref[...]


def _expert_down(te, nu, act, w_down, b_down):
    p, ff = act.shape
    d = w_down.shape[2]
    tm = MOE_TM
    tile_of = lambda i, nu: jnp.minimum(i, nu[0] - 1)
    grid_spec = pltpu.PrefetchScalarGridSpec(
        num_scalar_prefetch=2,
        grid=(p // tm,),
        in_specs=[pl.BlockSpec((tm, ff), lambda i, te, nu: (tile_of(i, nu), 0)),
                  pl.BlockSpec((None, ff, d), lambda i, te, nu: (te[i], 0, 0)),
                  pl.BlockSpec((None, 1, d), lambda i, te, nu: (te[i], 0, 0))],
        out_specs=pl.BlockSpec((tm, d), lambda i, te, nu: (tile_of(i, nu), 0)),
    )
    return pl.pallas_call(
        _expert_down_kernel,
        grid_spec=grid_spec,
        out_shape=jax.ShapeDtypeStruct((p, d), F32),
        compiler_params=_cparams("arbitrary"),
        name="expert_down",
    )(te, nu, act, w_down, b_down)


def _combine_kernel(pos_ref, x_ref, gt_ref, g_ref, b_ref, eo_ref, o_ref, ob_ref, buf_ref, sem, *, alpha):
    tc = x_ref.shape[0]

    def issue(r, carry):
        for k in range(TOP_K):
            _row_copy(eo_ref, pos_ref[0, 0, r * TOP_K + k], buf_ref.at[k], r, sem).start()
        return carry

    lax.fori_loop(0, tc, issue, 0)

    def drain(r, carry):
        for k in range(TOP_K):
            _row_copy(eo_ref, 0, buf_ref.at[k], 0, sem).wait()
        return carry

    lax.fori_loop(0, tc, drain, 0)

    gt = gt_ref[...]
    y = alpha * x_ref[...]
    for k in range(TOP_K):
        y = y + gt[:, k:k + 1] * buf_ref[k]
    mu = jnp.mean(y, axis=-1, keepdims=True)
    yc = y - mu
    var = jnp.mean(yc * yc, axis=-1, keepdims=True)
    out = yc * lax.rsqrt(var + 1e-5) * g_ref[...] + b_ref[...]
    o_ref[...] = out
    ob_ref[...] = out.astype(ob_ref.dtype)


def _combine_ln(pos, x1, gates, g, b, eo, alpha, tile=256):
    n, d = x1.shape
    pos3 = pos.reshape(n // tile, 1, tile * TOP_K)
    row = lambda w: pl.BlockSpec((1, w), lambda i: (0, 0))
    return pl.pallas_call(
        functools.partial(_combine_kernel, alpha=alpha),
        grid=(n // tile,),
        in_specs=[pl.BlockSpec((1, 1, tile * TOP_K), lambda i: (i, 0, 0), memory_space=pltpu.SMEM),
                  pl.BlockSpec((tile, d), lambda i: (i, 0)),
                  pl.BlockSpec((tile, LANES), lambda i: (i, 0)),
                  row(d), row(d),
                  pl.BlockSpec(memory_space=pl.ANY)],
        out_specs=[pl.BlockSpec((tile, d), lambda i: (i, 0)),
                   pl.BlockSpec((tile, d), lambda i: (i, 0))],
        out_shape=[jax.ShapeDtypeStruct((n, d), F32), jax.ShapeDtypeStruct((n, d), BF16)],
        scratch_shapes=[pltpu.VMEM((TOP_K, tile, d), F32), pltpu.SemaphoreType.DMA(())],
        compiler_params=_cparams("arbitrary"),
        name="moe_combine_ln",
    )(pos3, x1, gates, g, b, eo)


def _slot_positions(top_i, n_tiles):
    tm = MOE_TM
    e_flat = top_i.reshape(-1)
    onehot = (e_flat[:, None] == jnp.arange(N_EXPERTS, dtype=I32)[None, :]).astype(I32)
    csum = jnp.cumsum(onehot, axis=0)
    rank = jnp.sum(csum * onehot, axis=1) - 1
    counts = csum[-1]
    padded = ((counts + tm - 1) // tm) * tm
    ends = jnp.cumsum(padded)
    pos = (ends - padded)[e_flat] + rank
    n_used = (ends[-1] // tm).astype(I32)
    tile_e = jnp.searchsorted(ends, jnp.arange(n_tiles, dtype=I32) * tm, side="right").astype(I32)
    last_e = jnp.max(jnp.where(counts > 0, jnp.arange(N_EXPERTS, dtype=I32), 0))
    tile_e = jnp.minimum(tile_e, last_e)
    return pos.astype(I32), tile_e, n_used.reshape(1)


def _in_proj_columns():
    gw = GROUP_WIDTH
    gm, ssm = 0, 2 * gw
    xbc_w = gw + 4 * SSM_STATE
    ml = ssm + gw + xbc_w + SSM_HEADS
    att = ml + 4 * gw + 2 * ML_HEADS
    r = lambda a, b: list(range(a, b))
    big = (r(gm, gm + 2 * gw) + r(ssm + gw, ssm + gw + xbc_w) + r(ssm, ssm + gw) + r(ml, ml + 4 * gw)
           + r(att + Q_RANK, att + Q_RANK + 2 * gw) + r(att, att + Q_RANK))
    kidx0 = att + Q_RANK + 2 * gw
    small = {0: r(kidx0, kidx0 + IDX_DIM + IDX_HEADS),
             LANES: r(ssm + gw + xbc_w, ssm + gw + xbc_w + SSM_HEADS),
             2 * LANES: r(ml + 4 * gw, ml + 4 * gw + 2 * ML_HEADS)}
    return np.asarray(big, np.int32), small


def _pad_row(v, width, fill=0.0):
    v = v.reshape(1, -1).astype(F32)
    return jnp.pad(v, ((0, 0), (0, width - v.shape[1])), constant_values=fill)


def _rope_rows():
    def rows(rot, starts):
        half = rot // 2
        inv = ROPE_THETA ** (-jnp.arange(half, dtype=F32) * 2.0 / rot)
        f = jnp.zeros((LANES,), F32)
        s = jnp.zeros((LANES,), F32)
        for st in starts:
            f = f.at[st:st + half].set(inv).at[st + half:st + rot].set(inv)
            s = s.at[st:st + half].set(-1.0).at[st + half:st + rot].set(1.0)
        return f.reshape(1, LANES), s.reshape(1, LANES)
    fm, sm = rows(ATT_DIM // 4, [0])
    fi, si = rows(IDX_DIM // 4, [0, IDX_DIM])
    fk, sk = rows(IDX_DIM // 4, [0])
    return fm, sm, fi, si, fk, sk


def kernel(x, positions, w_in, gm_ln_g, gm_ln_b, gm_ws, gm_bs, ssm_conv_w, ssm_conv_b, ssm_dt_bias, ssm_a_log,
           ssm_d, ssm_norm_g, ml_b_i, ml_b_f, ml_norm_g, att_q_norm_g, att_w_uq, idx_k_ln_g, idx_k_ln_b, w_out,
           ln1_g, ln1_b, ln2_g, ln2_b, router_w, router_b, expert_w_gu, expert_b_gu, expert_w_down,
           expert_b_down):
    bsz, seq, d = x.shape
    depth = w_in.shape[0]
    n = bsz * seq
    alpha = (2.0 * depth) ** 0.25
    top_k = min(IDX_TOPK_MAX, seq // 4)
    n_slots = n * TOP_K + N_EXPERTS * MOE_TM
    n_tiles = n_slots // MOE_TM
    big_cols, small_cols = _in_proj_columns()
    rope_rows = _rope_rows()
    pos = positions.reshape(n, 1).astype(I32)

    xf = x.reshape(n, d)
    xb = xf.astype(BF16)
    for l in range(depth):
        w_big = jnp.pad(jnp.take(w_in[l], big_cols, axis=1), ((0, 0), (0, H_COLS - big_cols.size))).astype(BF16)
        w_small = jnp.zeros((d, HS_COLS), F32)
        for c0, cols in small_cols.items():
            w_small = w_small.at[:, c0:c0 + len(cols)].set(jnp.take(w_in[l], np.asarray(cols, np.int32), axis=1))
        h = _matmul(xb, w_big, BF16, 1024, 1024)
        hs = _matmul(xb, w_small.astype(BF16), F32, 1024, HS_COLS)

        y_gm = _gmlp(h, gm_ln_g[l].reshape(1, -1), gm_ln_b[l].reshape(1, -1), gm_ws[l],
                     gm_bs[l].reshape(GM_HEADS, GM_BLOCK, 1))
        y_ssm = _ssd(h, hs, bsz, seq, ssm_conv_w[l], ssm_conv_b[l].reshape(1, -1),
                     _pad_row(ssm_dt_bias[l], LANES), _pad_row(-jnp.exp(ssm_a_log[l].astype(F32)), LANES),
                     jnp.repeat(ssm_d[l].astype(F32), SSM_HEADDIM).reshape(1, -1), ssm_norm_g[l].reshape(1, -1))
        y_ml = _mlstm(h, hs, bsz, seq, _pad_row(jnp.concatenate([ml_b_i[l], ml_b_f[l]]), LANES),
                      ml_norm_g[l].reshape(1, -1))
        q, k, qi, ki, wi = _att_prep(h, hs, pos, att_q_norm_g[l].reshape(1, -1), att_w_uq[l].astype(BF16),
                                     _pad_row(idx_k_ln_g[l], LANES), _pad_row(idx_k_ln_b[l], LANES), rope_rows)
        thr, jst = _dsa_thresholds(qi, wi, ki, bsz, seq, top_k)
        y_att = _dsa_attention(q, qi, wi, thr, jst, k, h, ki, bsz, seq)

        x1, top_i, gates = _mix_ln_router(
            (y_gm, y_ssm, y_ml, y_att), xf, w_out[l].astype(BF16), ln1_g[l].reshape(1, -1), ln1_b[l].reshape(1, -1),
            jnp.pad(router_w[l], ((0, 0), (0, LANES - N_EXPERTS))),
            _pad_row(router_b[l], LANES, -jnp.inf), alpha)

        slot, tile_e, n_used = _slot_positions(top_i[:, :TOP_K], n_tiles)
        xs = _dispatch(slot, x1, n_slots)
        act = _expert_up(tile_e, n_used, xs, expert_w_gu[l].astype(BF16), expert_b_gu[l][:, None, :])
        eo = _expert_down(tile_e, n_used, act, expert_w_down[l].astype(BF16), expert_b_down[l][:, None, :])
        xf, xb = _combine_ln(slot, x1, gates, ln2_g[l].reshape(1, -1), ln2_b[l].reshape(1, -1), eo, alpha)
    return xf.reshape(bsz, seq, d)
```

```python
import functools
import math

import numpy as np
import jax
import jax.numpy as jnp
from jax import lax
from jax.experimental import pallas as pl
from jax.experimental.pallas import tpu as pltpu

F32 = jnp.float32
BF16 = jnp.bfloat16
I32 = jnp.int32

LANES = 128
CHUNK = 64
GROUP_WIDTH = 512
GM_BLOCK = 128
GM_HEADS = 4
SSM_HEADS = 8
SSM_HEADDIM = 64
SSM_STATE = 128
SSM_CONV = 4
ML_HEADS = 4
ML_DIM = 128
ATT_HEADS = 4
ATT_DIM = 128
Q_RANK = 384
IDX_HEADS = 8
IDX_DIM = 64
IDX_TOPK_MAX = 256
ROPE_THETA = 500000.0
N_EXPERTS = 32
TOP_K = 4
SWIGLU_LIMIT = 7.0
SWIGLU_ALPHA = 1.702

SEQ_CHUNK = 128
SEQ_TILE = 512
DSA_SB = 128
DSA_QB = 256
DSA_TK = 512
MOE_TM = 512
VMEM_LIMIT = 56 * 1024 * 1024

NEG_BIG = -1e30
INT_MIN = -(2 ** 31)
INT_MAX = 2 ** 31 - 1
NEG_INF_KEY = int(np.int32(np.uint32(0xFF800000) ^ np.uint32(0x7FFFFFFF)))

H_GM, H_XBC, H_Z, H_Q, H_K, H_V, H_O, H_AK, H_AV, H_CQ, H_COLS = (
    0, 1024, 2048, 2560, 3072, 3584, 4096, 4608, 5120, 5632, 6144)
HS_COLS = 384


def _cparams(*sem):
    return pltpu.CompilerParams(dimension_semantics=sem, vmem_limit_bytes=VMEM_LIMIT)


def _dot(a, b):
    return jnp.dot(a, b, preferred_element_type=F32)


def _dot_nt(a, b):
    return lax.dot_general(a, b, (((1,), (1,)), ((), ())), preferred_element_type=F32)


def _dot_exact_lhs(tri, x):
    hi = x.astype(BF16)
    r1 = x - hi.astype(F32)
    mid = r1.astype(BF16)
    lo = (r1 - mid.astype(F32)).astype(BF16)
    return _dot(tri, hi) + _dot(tri, mid) + _dot(tri, lo)


def _lower_tri(n):
    r = lax.broadcasted_iota(I32, (n, n), 0)
    c = lax.broadcasted_iota(I32, (n, n), 1)
    return r >= c


def _sigmoid(x):
    return 1.0 / (1.0 + jnp.exp(-x))


def _softplus(x):
    return jnp.maximum(x, 0.0) + jnp.log(1.0 + jnp.exp(-jnp.abs(x)))


def _log_sigmoid(x):
    return -_softplus(-x)


def _mm_kernel(a_ref, b_ref, o_ref):
    o_ref[...] = _dot(a_ref[...], b_ref[...]).astype(o_ref.dtype)


def _matmul(a, b, out_dtype, tm, tn):
    m, k = a.shape
    n = b.shape[1]
    return pl.pallas_call(
        _mm_kernel,
        grid=(n // tn, m // tm),
        in_specs=[pl.BlockSpec((tm, k), lambda j, i: (i, 0)),
                  pl.BlockSpec((k, tn), lambda j, i: (0, j))],
        out_specs=pl.BlockSpec((tm, tn), lambda j, i: (i, j)),
        out_shape=jax.ShapeDtypeStruct((m, n), out_dtype),
        compiler_params=_cparams("parallel", "parallel"),
        name="in_proj",
    )(a, b)


def _gmlp_kernel(h_ref, lng_ref, lnb_ref, ws_ref, bs_ref, o_ref):
    t = h_ref.shape[0]
    h = h_ref[...].astype(F32)
    h = 0.5 * h * (1.0 + lax.erf(h * (1.0 / math.sqrt(2.0))))
    r = lax.broadcasted_iota(I32, (GM_BLOCK, GM_BLOCK), 0)
    c = lax.broadcasted_iota(I32, (GM_BLOCK, GM_BLOCK), 1)
    allowed = (c // CHUNK) <= (r // CHUNK)
    for g in range(GM_HEADS):
        u = h[:, g * LANES:(g + 1) * LANES]
        v = h[:, GROUP_WIDTH + g * LANES:GROUP_WIDTH + (g + 1) * LANES]
        mu = jnp.mean(v, axis=-1, keepdims=True)
        vc = v - mu
        var = jnp.mean(vc * vc, axis=-1, keepdims=True)
        vn = vc * lax.rsqrt(var + 1e-5) * lng_ref[:, g * LANES:(g + 1) * LANES] \
            + lnb_ref[:, g * LANES:(g + 1) * LANES]
        w = jnp.where(allowed, ws_ref[g], 0.0).astype(BF16)
        for wdw in range(t // GM_BLOCK):
            rows = slice(wdw * GM_BLOCK, (wdw + 1) * GM_BLOCK)
            vmix = _dot(w, vn[rows].astype(BF16)) + bs_ref[g]
            o_ref[rows, g * LANES:(g + 1) * LANES] = (u[rows] * vmix).astype(o_ref.dtype)


def _gmlp(h, lng, lnb, ws, bs, tile=256):
    n = h.shape[0]
    return pl.pallas_call(
        _gmlp_kernel,
        grid=(n // tile,),
        in_specs=[pl.BlockSpec((tile, 2 * GROUP_WIDTH), lambda i: (i, H_GM // (2 * GROUP_WIDTH))),
                  pl.BlockSpec((1, GROUP_WIDTH), lambda i: (0, 0)),
                  pl.BlockSpec((1, GROUP_WIDTH), lambda i: (0, 0)),
                  pl.BlockSpec((GM_HEADS, GM_BLOCK, GM_BLOCK), lambda i: (0, 0, 0)),
                  pl.BlockSpec((GM_HEADS, GM_BLOCK, 1), lambda i: (0, 0, 0))],
        out_specs=pl.BlockSpec((tile, GROUP_WIDTH), lambda i: (i, 0)),
        out_shape=jax.ShapeDtypeStruct((n, GROUP_WIDTH), BF16),
        compiler_params=_cparams("parallel"),
        name="gmlp",
    )(h, lng, lnb, ws, bs)


def _ssd_kernel(xbc_ref, z_ref, dt_ref, cw_ref, cb_ref, dtb_ref, a_ref, dsk_ref, ng_ref, o_ref,
                buf_ref, act_ref, st_ref):
    t = xbc_ref.shape[0]
    lc = SEQ_CHUNK

    @pl.when(pl.program_id(1) == 0)
    def _():
        buf_ref[0:8, :] = jnp.zeros((8, buf_ref.shape[1]), F32)
        st_ref[...] = jnp.zeros(st_ref.shape, F32)

    buf_ref[8:8 + t, :] = xbc_ref[...].astype(F32)
    conv = cb_ref[...] + cw_ref[0:1, :] * buf_ref[5:5 + t, :]
    for k in range(1, SSM_CONV):
        conv = conv + cw_ref[k:k + 1, :] * buf_ref[5 + k:5 + k + t, :]
    act_ref[...] = conv * _sigmoid(conv)
    buf_ref[0:8, :] = buf_ref[t:t + 8, :]

    tri = _lower_tri(lc)
    tri_b = tri.astype(BF16)
    lane = lax.broadcasted_iota(I32, (lc, LANES), 1)
    first_half = lane < SSM_HEADDIM
    a_row = a_ref[...]

    def chunk(ci, carry):
        r0 = pl.multiple_of(ci * lc, lc)
        xa = act_ref[pl.ds(r0, lc), :]
        dt = _softplus(dt_ref[pl.ds(r0, lc), :] + dtb_ref[...])
        acs = _dot_exact_lhs(tri_b, dt * a_row)
        acs_t = acs.T
        ys = []
        for p in range(SSM_HEADS // 2):
            g = p // 2
            h0, h1 = 2 * p, 2 * p + 1
            bm = xa[:, GROUP_WIDTH + g * SSM_STATE:GROUP_WIDTH + (g + 1) * SSM_STATE]
            cm = xa[:, GROUP_WIDTH + 2 * SSM_STATE + g * SSM_STATE:GROUP_WIDTH + 2 * SSM_STATE + (g + 1) * SSM_STATE]
            cm_b = cm.astype(BF16)
            cb = _dot_nt(cm_b, bm.astype(BF16))
            x2 = xa[:, p * LANES:(p + 1) * LANES]
            dt2 = jnp.where(first_half, dt[:, h0:h0 + 1], dt[:, h1:h1 + 1])
            xdt = x2 * dt2
            xdt_b = xdt.astype(BF16)
            yd = []
            for hh in (h0, h1):
                decay = jnp.where(tri, jnp.exp(acs[:, hh:hh + 1] - acs_t[hh:hh + 1, :]), 0.0)
                yd.append(_dot((cb * decay).astype(BF16), xdt_b))
            y_diag = jnp.where(first_half, yd[0], yd[1])
            acs2 = jnp.where(first_half, acs[:, h0:h0 + 1], acs[:, h1:h1 + 1])
            aend2 = jnp.where(first_half[0:1], acs[lc - 1:lc, h0:h0 + 1], acs[lc - 1:lc, h1:h1 + 1])
            st = st_ref[p]
            y_off = _dot(cm_b, st.astype(BF16)) * jnp.exp(acs2)
            ys.append(y_diag + y_off + x2 * dsk_ref[:, p * LANES:(p + 1) * LANES])
            upd = _dot(bm.T.astype(BF16), (xdt * jnp.exp(aend2 - acs2)).astype(BF16))
            st_ref[p] = jnp.exp(aend2) * st + upd
        y = jnp.concatenate(ys, axis=-1)
        zz = z_ref[pl.ds(r0, lc), :].astype(F32)
        y = y * (zz * _sigmoid(zz))
        y = y * lax.rsqrt(jnp.mean(y * y, axis=-1, keepdims=True) + 1e-6) * ng_ref[...]
        o_ref[pl.ds(r0, lc), :] = y.astype(o_ref.dtype)
        return carry

    lax.fori_loop(0, t // lc, chunk, 0)


def _ssd(h, hs, bsz, seq, cw, cb, dtb, a_row, dsk, ng, tile=SEQ_TILE):
    n = h.shape[0]
    nt = seq // tile
    xw = 2 * GROUP_WIDTH
    return pl.pallas_call(
        _ssd_kernel,
        grid=(bsz, nt),
        in_specs=[pl.BlockSpec((tile, xw), lambda b, j: (b * nt + j, H_XBC // xw)),
                  pl.BlockSpec((tile, GROUP_WIDTH), lambda b, j: (b * nt + j, H_Z // GROUP_WIDTH)),
                  pl.BlockSpec((tile, LANES), lambda b, j: (b * nt + j, 1)),
                  pl.BlockSpec((SSM_CONV, xw), lambda b, j: (0, 0)),
                  pl.BlockSpec((1, xw), lambda b, j: (0, 0)),
                  pl.BlockSpec((1, LANES), lambda b, j: (0, 0)),
                  pl.BlockSpec((1, LANES), lambda b, j: (0, 0)),
                  pl.BlockSpec((1, GROUP_WIDTH), lambda b, j: (0, 0)),
                  pl.BlockSpec((1, GROUP_WIDTH), lambda b, j: (0, 0))],
        out_specs=pl.BlockSpec((tile, GROUP_WIDTH), lambda b, j: (b * nt + j, 0)),
        out_shape=jax.ShapeDtypeStruct((n, GROUP_WIDTH), BF16),
        scratch_shapes=[pltpu.VMEM((tile + 8, xw), F32),
                        pltpu.VMEM((tile, xw), F32),
                        pltpu.VMEM((SSM_HEADS // 2, SSM_STATE, LANES), F32)],
        compiler_params=_cparams("arbitrary", "arbitrary"),
        name="ssd",
    )(h, h, hs, cw, cb, dtb, a_row, dsk, ng)


def _mlstm_kernel(q_ref, k_ref, v_ref, og_ref, gt_ref, bias_ref, ng_ref, o_ref, c_ref, n_ref, m_ref):
    t = q_ref.shape[0]
    lc = SEQ_CHUNK
    scale = ML_DIM ** -0.5

    @pl.when(pl.program_id(1) == 0)
    def _():
        c_ref[...] = jnp.zeros(c_ref.shape, F32)
        n_ref[...] = jnp.zeros(n_ref.shape, F32)
        m_ref[...] = jnp.zeros(m_ref.shape, F32)

    tri = _lower_tri(lc)
    tri_b = tri.astype(BF16)

    def chunk(ci, carry):
        r0 = pl.multiple_of(ci * lc, lc)
        gts = gt_ref[pl.ds(r0, lc), :] + bias_ref[...]
        bcum = _dot_exact_lhs(tri_b, _log_sigmoid(gts))
        gts_t = gts.T
        bcum_t = bcum.T
        for hd in range(ML_HEADS):
            cols = slice(hd * ML_DIM, (hd + 1) * ML_DIM)
            qh = q_ref[pl.ds(r0, lc), cols]
            kh = (k_ref[pl.ds(r0, lc), cols].astype(F32) * scale).astype(BF16)
            vh = v_ref[pl.ds(r0, lc), cols]
            i_col = gts[:, hd:hd + 1]
            i_row = gts_t[hd:hd + 1, :]
            b_col = bcum[:, ML_HEADS + hd:ML_HEADS + hd + 1]
            b_row = bcum_t[ML_HEADS + hd:ML_HEADS + hd + 1, :]
            b_end = b_col[lc - 1:lc, :]
            m_prev = m_ref[hd][0:1, 0:1]
            c_prev = c_ref[hd]
            n_prev = n_ref[hd][0:1, :]
            log_d = jnp.where(tri, b_col - b_row + i_row, -jnp.inf)
            inter = b_col + m_prev
            m_j = jnp.maximum(jnp.max(log_d, axis=-1, keepdims=True), inter)
            s_mat = jnp.exp(log_d - m_j) * _dot_nt(qh, kh)
            gsc = jnp.exp(inter - m_j)
            num = _dot(s_mat.astype(BF16), vh) + gsc * _dot_nt(qh, c_prev.astype(BF16))
            den = jnp.sum(s_mat, axis=-1, keepdims=True) \
                + gsc * jnp.sum(qh.astype(F32) * n_prev, axis=-1, keepdims=True)
            hh = num / jnp.maximum(jnp.abs(den), jnp.exp(-m_j))
            mu = jnp.mean(hh, axis=-1, keepdims=True)
            hc = hh - mu
            var = jnp.mean(hc * hc, axis=-1, keepdims=True)
            hn = hc * lax.rsqrt(var + 1e-5) * ng_ref[:, cols]
            og = og_ref[pl.ds(r0, lc), cols].astype(F32)
            o_ref[pl.ds(r0, lc), cols] = (hn * _sigmoid(og)).astype(o_ref.dtype)
            a_col = b_end - b_col + i_col
            a_row = b_end - b_row + i_row
            m_new = jnp.maximum(b_end + m_prev, jnp.max(a_row, axis=-1, keepdims=True))
            decay = jnp.exp(b_end + m_prev - m_new)
            w_col = jnp.exp(a_col - m_new)
            kf = kh.astype(F32)
            vw_t = (vh.astype(F32) * w_col).T.astype(BF16)
            c_ref[hd] = decay * c_prev + _dot(vw_t, kh)
            n_new = decay * n_prev + jnp.sum(w_col * kf, axis=0, keepdims=True)
            n_ref[hd] = jnp.broadcast_to(n_new, (8, ML_DIM))
            m_ref[hd] = jnp.broadcast_to(m_new, (8, LANES))
        return carry

    lax.fori_loop(0, t // lc, chunk, 0)


def _mlstm(h, hs, bsz, seq, bias, ng, tile=SEQ_TILE):
    n = h.shape[0]
    nt = seq // tile
    gw = GROUP_WIDTH

    def col(c0):
        return pl.BlockSpec((tile, gw), lambda b, j: (b * nt + j, c0 // gw))

    return pl.pallas_call(
        _mlstm_kernel,
        grid=(bsz, nt),
        in_specs=[col(H_Q), col(H_K), col(H_V), col(H_O),
                  pl.BlockSpec((tile, LANES), lambda b, j: (b * nt + j, 2)),
                  pl.BlockSpec((1, LANES), lambda b, j: (0, 0)),
                  pl.BlockSpec((1, gw), lambda b, j: (0, 0))],
        out_specs=pl.BlockSpec((tile, gw), lambda b, j: (b * nt + j, 0)),
        out_shape=jax.ShapeDtypeStruct((n, gw), BF16),
        scratch_shapes=[pltpu.VMEM((ML_HEADS, ML_DIM, ML_DIM), F32),
                        pltpu.VMEM((ML_HEADS, 8, ML_DIM), F32),
                        pltpu.VMEM((ML_HEADS, 8, LANES), F32)],
        compiler_params=_cparams("arbitrary", "arbitrary"),
        name="mlstm",
    )(h, h, h, h, hs, bias, ng)


def _att_prep_kernel(cq_ref, ak_ref, hs_ref, pos_ref, qg_ref, wuq_ref, kg_ref, kb_ref,
                     fm_ref, sm_ref, fi_ref, si_ref, fk_ref, sk_ref,
                     q_ref, k_ref, qi_ref, ki_ref, w_ref):
    t = cq_ref.shape[0]
    lane = lax.broadcasted_iota(I32, (t, LANES), 1)
    pos = pos_ref[...].astype(F32)

    cq = cq_ref[:, 0:Q_RANK].astype(F32)
    cq = cq * lax.rsqrt(jnp.mean(cq * cq, axis=-1, keepdims=True) + 1e-6) * qg_ref[...]
    q_all = _dot(cq.astype(BF16), wuq_ref[...])

    def rope(x, cos, sin_signed, half, period):
        first = (lane % period) < half
        partner = jnp.where(first, pltpu.roll(x, LANES - half, 1), pltpu.roll(x, half, 1))
        return x * cos + partner * sin_signed

    ang = pos * fm_ref[...]
    cos_m, sin_m = jnp.cos(ang), jnp.sin(ang) * sm_ref[...]
    half_m = ATT_DIM // 8
    for hd in range(ATT_HEADS):
        cols = slice(hd * ATT_DIM, (hd + 1) * ATT_DIM)
        q_ref[:, cols] = (rope(q_all[:, cols], cos_m, sin_m, half_m, LANES) * (ATT_DIM ** -0.5)).astype(q_ref.dtype)
        k_ref[:, cols] = rope(ak_ref[:, cols].astype(F32), cos_m, sin_m, half_m, LANES).astype(k_ref.dtype)

    ang = pos * fi_ref[...]
    cos_i, sin_i = jnp.cos(ang), jnp.sin(ang) * si_ref[...]
    half_i = IDX_DIM // 8
    low = lane < IDX_DIM
    for pr in range(IDX_HEADS // 2):
        x = rope(q_all[:, GROUP_WIDTH + pr * LANES:GROUP_WIDTH + (pr + 1) * LANES], cos_i, sin_i, half_i, IDX_DIM)
        even = jnp.where(low, x, 0.0).astype(qi_ref.dtype)
        odd = jnp.where(low, pltpu.roll(x, IDX_DIM, 1), 0.0).astype(qi_ref.dtype)
        for blk in range(t // DSA_SB):
            rows = slice(blk * DSA_SB, (blk + 1) * DSA_SB)
            qi_ref[blk, 2 * pr] = even[rows]
            qi_ref[blk, 2 * pr + 1] = odd[rows]

    hs = hs_ref[...]
    kx = jnp.where(low, hs, 0.0)
    mu = jnp.sum(kx, axis=-1, keepdims=True) * (1.0 / IDX_DIM)
    kc = jnp.where(low, hs - mu, 0.0)
    var = jnp.sum(kc * kc, axis=-1, keepdims=True) * (1.0 / IDX_DIM)
    kn = kc * lax.rsqrt(var + 1e-5) * kg_ref[...] + kb_ref[...]
    ang = pos * fk_ref[...]
    ki_ref[...] = rope(kn, jnp.cos(ang), jnp.sin(ang) * sk_ref[...], half_i, LANES).astype(ki_ref.dtype)
    wi = jnp.where(lane < IDX_HEADS, pltpu.roll(hs, LANES - IDX_DIM, 1), 0.0) * (IDX_HEADS ** -0.5 * IDX_DIM ** -0.5)
    for blk in range(t // DSA_SB):
        w_ref[blk] = wi[blk * DSA_SB:(blk + 1) * DSA_SB, :].T[0:IDX_HEADS, :]


def _att_prep(h, hs, pos, qg, wuq, kg, kb, rope_rows, tile=256):
    n = h.shape[0]
    gw = GROUP_WIDTH
    row = lambda w: pl.BlockSpec((1, w), lambda i: (0, 0))
    return pl.pallas_call(
        _att_prep_kernel,
        grid=(n // tile,),
        in_specs=[pl.BlockSpec((tile, gw), lambda i: (i, H_CQ // gw)),
                  pl.BlockSpec((tile, gw), lambda i: (i, H_AK // gw)),
                  pl.BlockSpec((tile, LANES), lambda i: (i, 0)),
                  pl.BlockSpec((tile, 1), lambda i: (i, 0)),
                  row(Q_RANK),
                  pl.BlockSpec(wuq.shape, lambda i: (0, 0)),
                  row(LANES), row(LANES)] + [row(LANES)] * 6,
        out_specs=[pl.BlockSpec((tile, gw), lambda i: (i, 0)),
                   pl.BlockSpec((tile, gw), lambda i: (i, 0)),
                   pl.BlockSpec((tile // DSA_SB, IDX_HEADS, DSA_SB, LANES), lambda i: (i, 0, 0, 0)),
                   pl.BlockSpec((tile, LANES), lambda i: (i, 0)),
                   pl.BlockSpec((tile // DSA_SB, IDX_HEADS, DSA_SB), lambda i: (i, 0, 0))],
        out_shape=[jax.ShapeDtypeStruct((n, gw), BF16),
                   jax.ShapeDtypeStruct((n, gw), BF16),
                   jax.ShapeDtypeStruct((n // DSA_SB, IDX_HEADS, DSA_SB, LANES), BF16),
                   jax.ShapeDtypeStruct((n, LANES), BF16),
                   jax.ShapeDtypeStruct((n // DSA_SB, IDX_HEADS, DSA_SB), F32)],
        compiler_params=_cparams("parallel"),
        name="att_prep",
    )(h, h, hs, pos, qg, wuq, kg, kb, *rope_rows)


def _dsa_select_kernel(qi_ref, w_ref, ki_ref, bias_ref, key_ref, hi_ref, lo_ref, *, top_k, idx_bits):
    sb = w_ref.shape[2]
    nt_all, tk, _ = key_ref.shape
    i = pl.program_id(1)
    n_tiles = (i * sb + sb + tk - 1) // tk
    row = lax.broadcasted_iota(I32, (1, sb), 1)
    lim = ((i * sb + row) // CHUNK + 1) * CHUNK
    sub_t = lax.broadcasted_iota(I32, (tk, sb), 0)
    qi = qi_ref[...]
    w = w_ref[0]
    half_min = -(2 ** 15)

    def fill(kt, carry):
        k0 = pl.multiple_of(kt * tk, tk)
        rel = _dot_nt(ki_ref[pl.ds(k0, tk), :], qi)
        acc = None
        for hd in range(IDX_HEADS):
            term = jnp.maximum(rel[:, hd * sb:(hd + 1) * sb], 0.0) * w[hd:hd + 1, :]
            acc = term if acc is None else acc + term
        score = jnp.where(sub_t + k0 < lim, acc + 0.0, -jnp.inf)
        bits = pltpu.bitcast(score, I32)
        keys = bits ^ ((bits >> 31) & INT_MAX)
        key_ref[kt] = keys
        hi_ref[kt] = (keys >> 16).astype(jnp.int16)
        return carry

    lax.fori_loop(0, n_tiles, fill, 0)

    def tree_sum(parts):
        while len(parts) > 1:
            parts = [parts[j] + parts[j + 1] for j in range(0, len(parts), 2)]
        return parts[0]

    def count(ref, pred, dtype):
        gran = 32 // jnp.dtype(dtype).itemsize

        def body(kt, acc):
            m = pred(ref[kt], kt).astype(dtype)
            return acc + tree_sum([m[c * gran:(c + 1) * gran] for c in range(tk // gran)])
        acc = lax.fori_loop(0, n_tiles, body, jnp.zeros((gran, sb), dtype))
        return jnp.sum(acc.astype(I32), axis=0, keepdims=True)

    def count16_ge(ref, cand):
        c16 = cand.astype(jnp.int16)
        return count(ref, lambda v, kt: v >= c16, jnp.int16)

    def greedy16(ref, need):
        def step(s, t):
            cand = t + lax.shift_left(jnp.int32(1), 15 - s)
            return jnp.where(count16_ge(ref, cand) >= need, cand, t)
        return lax.fori_loop(0, 16, step, jnp.full((1, sb), half_min, I32))

    t_hi = greedy16(hi_ref, top_k)
    t_hi16 = t_hi.astype(jnp.int16)
    above = count(hi_ref, lambda v, kt: v > t_hi16, jnp.int16)
    tied_hi = count(hi_ref, lambda v, kt: v == t_hi16, jnp.int16)
    need_lo = top_k - above

    def fill_lo(kt, carry):
        lo = ((key_ref[kt] & 0xFFFF) + half_min).astype(jnp.int16)
        lo_ref[kt] = jnp.where(hi_ref[kt] == t_hi16, lo, jnp.int16(half_min))
        return carry

    lax.fori_loop(0, n_tiles, fill_lo, 0)
    t_lo = greedy16(lo_ref, need_lo)
    thr = (t_hi << 16) | ((t_lo - half_min) & 0xFFFF)
    at_least = above + jnp.where(t_lo == half_min, tied_hi, count16_ge(lo_ref, t_lo))
    excess = (at_least > top_k) & (thr > NEG_INF_KEY)

    def emit(bound):
        def body(kt, carry):
            keys = key_ref[kt]
            sidx = sub_t + kt * tk
            sel = ((keys > thr) | ((keys == thr) & (sidx < bound))) & (sidx < lim)
            bias_t = jnp.where(sel, 0.0, -jnp.inf)
            for c in range(tk // LANES):
                bias_ref[0, kt, :, c * LANES:(c + 1) * LANES] = \
                    bias_t[c * LANES:(c + 1) * LANES, :].T.astype(bias_ref.dtype)
            return carry
        lax.fori_loop(0, n_tiles, body, 0)

    any_excess = jnp.max(excess.astype(I32)) > 0

    @pl.when(jnp.logical_not(any_excess))
    def _():
        emit(jnp.full((1, sb), INT_MAX, I32))

    @pl.when(any_excess)
    def _():
        need = top_k - count(key_ref, lambda v, kt: v > thr, I32)

        def idx_step(it, bound):
            cand = bound | lax.shift_left(jnp.int32(1), idx_bits - 1 - it)
            below = count(key_ref, lambda v, kt: (v == thr) & (sub_t + kt * tk < cand), I32)
            return jnp.where(below <= need, cand, bound)

        bound = lax.fori_loop(0, idx_bits, idx_step, jnp.zeros((1, sb), I32))
        emit(jnp.where(excess, bound, INT_MAX))

    def blank(kt, carry):
        bias_ref[0, kt] = jnp.full(bias_ref.shape[2:], -jnp.inf, bias_ref.dtype)
        return carry

    lax.fori_loop(n_tiles, nt_all, blank, 0)


def _dsa_select(qi, w, ki, bsz, seq, top_k):
    nb, _, sb = w.shape
    tk = DSA_TK
    nq, nt = seq // sb, seq // tk
    kern = functools.partial(_dsa_select_kernel, top_k=top_k, idx_bits=int(seq).bit_length())
    return pl.pallas_call(
        kern,
        grid=(bsz, nq),
        in_specs=[pl.BlockSpec((IDX_HEADS * sb, LANES), lambda b, i: (b * nq + i, 0)),
                  pl.BlockSpec((1, IDX_HEADS, sb), lambda b, i: (b * nq + i, 0, 0)),
                  pl.BlockSpec((seq, LANES), lambda b, i: (b, 0))],
        out_specs=pl.BlockSpec((1, nt, sb, tk), lambda b, i: (b * nq + i, 0, 0, 0)),
        out_shape=jax.ShapeDtypeStruct((nb, nt, sb, tk), BF16),
        scratch_shapes=[pltpu.VMEM((nt, tk, sb), I32),
                        pltpu.VMEM((nt, tk, sb), jnp.int16),
                        pltpu.VMEM((nt, tk, sb), jnp.int16)],
        compiler_params=_cparams("parallel", "parallel"),
        name="dsa_select",
    )(qi.reshape(nb * IDX_HEADS * sb, LANES), w, ki)


def _dsa_attn_kernel(qt_ref, kt_ref, first_ref, last_ref, q_ref, bias_ref, k_ref, v_ref, o_ref,
                     m_ref, l_ref, acc_ref):
    s = pl.program_id(1)
    qb, tk = q_ref.shape[0], k_ref.shape[0]

    @pl.when(first_ref[s] == 1)
    def _():
        m_ref[...] = jnp.full(m_ref.shape, NEG_BIG, F32)
        l_ref[...] = jnp.zeros(l_ref.shape, F32)
        acc_ref[...] = jnp.zeros(acc_ref.shape, F32)

    bias = bias_ref[...].reshape(qb, tk).astype(F32)

    for hd in range(ATT_HEADS):
        cols = slice(hd * ATT_DIM, (hd + 1) * ATT_DIM)
        logits = _dot_nt(q_ref[:, cols], k_ref[:, cols]) + bias
        m_prev = m_ref[:, cols]
        m_new = jnp.maximum(m_prev, jnp.max(logits, axis=-1, keepdims=True))
        p = jnp.exp(logits - m_new[:, 0:1])
        alpha = jnp.exp(m_prev - m_new)
        l_ref[:, cols] = alpha * l_ref[:, cols] + jnp.sum(p, axis=-1, keepdims=True)
        acc_ref[:, cols] = alpha * acc_ref[:, cols] + _dot(p.astype(BF16), v_ref[:, cols])
        m_ref[:, cols] = m_new

    @pl.when(last_ref[s] == 1)
    def _():
        o_ref[...] = (acc_ref[...] / l_ref[...]).astype(o_ref.dtype)


def _dsa_attention(q, bias, k, h, bsz, seq):
    n = q.shape[0]
    qb, tk = DSA_QB, DSA_TK
    nq, nkt = seq // qb, seq // tk
    sub = qb // DSA_SB
    pairs = [(i, t) for i in range(nq) for t in range((i * qb + qb - 1) // tk + 1)]
    qt = jnp.asarray([p[0] for p in pairs], I32)
    ktab = jnp.asarray([p[1] for p in pairs], I32)
    first = jnp.asarray([1 if p[1] == 0 else 0 for p in pairs], I32)
    last = jnp.asarray([1 if p[1] == (p[0] * qb + qb - 1) // tk else 0 for p in pairs], I32)
    gw = GROUP_WIDTH

    def qmap(w_):
        return pl.BlockSpec((qb, w_), lambda b, s, qt, kt, f, l: (b * nq + qt[s], 0))

    def kmap(w_, cb):
        return pl.BlockSpec((tk, w_), lambda b, s, qt, kt, f, l: (b * nkt + kt[s], cb))

    grid_spec = pltpu.PrefetchScalarGridSpec(
        num_scalar_prefetch=4,
        grid=(bsz, len(pairs)),
        in_specs=[qmap(gw),
                  pl.BlockSpec((sub, 1, DSA_SB, tk), lambda b, s, qt, kt, f, l: (b * nq + qt[s], kt[s], 0, 0)),
                  kmap(gw, 0), kmap(gw, H_AV // gw)],
        out_specs=qmap(gw),
        scratch_shapes=[pltpu.VMEM((qb, gw), F32), pltpu.VMEM((qb, gw), F32), pltpu.VMEM((qb, gw), F32)],
    )
    return pl.pallas_call(
        _dsa_attn_kernel,
        grid_spec=grid_spec,
        out_shape=jax.ShapeDtypeStruct((n, gw), BF16),
        compiler_params=_cparams("arbitrary", "arbitrary"),
        name="dsa_attention",
    )(qt, ktab, first, last, q, bias, k, h)


def _dsa(h, hs, pos, bsz, seq, q_norm_g, w_uq, idxk_g, idxk_b):
    q, k, qi, ki, wi = _att_prep(h, hs, pos, q_norm_g.reshape(1, -1), w_uq.astype(BF16),
                                 _pad_row(idxk_g, LANES), _pad_row(idxk_b, LANES), _rope_rows())
    bias = _dsa_select(qi, wi, ki, bsz, seq, min(IDX_TOPK_MAX, seq // 4))
    return _dsa_attention(q, bias, k, h, bsz, seq)


def _mix_kernel(ya_ref, yb_ref, yc_ref, yd_ref, x_ref, wo_ref, g_ref, b_ref, rw_ref, rb_ref,
                x1_ref, ti_ref, gt_ref, *, alpha):
    gw = GROUP_WIDTH
    mix = _dot(ya_ref[...], wo_ref[0:gw, :])
    mix = mix + _dot(yb_ref[...], wo_ref[gw:2 * gw, :])
    mix = mix + _dot(yc_ref[...], wo_ref[2 * gw:3 * gw, :])
    mix = mix + _dot(yd_ref[...], wo_ref[3 * gw:4 * gw, :])
    y = alpha * x_ref[...] + mix
    mu = jnp.mean(y, axis=-1, keepdims=True)
    yc = y - mu
    var = jnp.mean(yc * yc, axis=-1, keepdims=True)
    x1 = yc * lax.rsqrt(var + 1e-5) * g_ref[...] + b_ref[...]
    x1_ref[...] = x1

    logits = jnp.dot(x1, rw_ref[...], precision=lax.Precision.HIGHEST, preferred_element_type=F32) + rb_ref[...]
    lane = lax.broadcasted_iota(I32, logits.shape, 1)
    vals, idxs = [], []
    for _ in range(TOP_K):
        mx = jnp.max(logits, axis=-1, keepdims=True)
        ix = jnp.min(jnp.where(logits == mx, lane, LANES), axis=-1, keepdims=True)
        vals.append(mx)
        idxs.append(ix)
        logits = jnp.where(lane == ix, -jnp.inf, logits)
    es = [jnp.exp(v - vals[0]) for v in vals]
    tot = es[0] + es[1] + es[2] + es[3]
    ti = jnp.zeros(logits.shape, I32)
    gt = jnp.zeros(logits.shape, F32)
    for k in range(TOP_K):
        ti = jnp.where(lane == k, idxs[k], ti)
        gt = jnp.where(lane == k, es[k] / tot, gt)
    ti_ref[...] = ti
    gt_ref[...] = gt


def _mix_ln_router(ys, x, wo, g, b, rw, rb, alpha, tile=256):
    n, d = x.shape
    gw = GROUP_WIDTH
    yspec = pl.BlockSpec((tile, gw), lambda i: (i, 0))
    row = lambda w: pl.BlockSpec((1, w), lambda i: (0, 0))
    return pl.pallas_call(
        functools.partial(_mix_kernel, alpha=alpha),
        grid=(n // tile,),
        in_specs=[yspec, yspec, yspec, yspec,
                  pl.BlockSpec((tile, d), lambda i: (i, 0)),
                  pl.BlockSpec(wo.shape, lambda i: (0, 0)),
                  row(d), row(d),
                  pl.BlockSpec(rw.shape, lambda i: (0, 0)),
                  row(LANES)],
        out_specs=[pl.BlockSpec((tile, d), lambda i: (i, 0)),
                   pl.BlockSpec((tile, LANES), lambda i: (i, 0)),
                   pl.BlockSpec((tile, LANES), lambda i: (i, 0))],
        out_shape=[jax.ShapeDtypeStruct((n, d), F32),
                   jax.ShapeDtypeStruct((n, LANES), I32),
                   jax.ShapeDtypeStruct((n, LANES), F32)],
        compiler_params=_cparams("parallel"),
        name="mix_ln_router",
    )(*ys, x, wo, g, b, rw, rb)


def _row_copy(src, s, dst, d, sem):
    return pltpu.make_async_copy(src.at[pl.ds(s, 1), :], dst.at[pl.ds(d, 1), :], sem)


def _dispatch_kernel(pos_ref, x_ref, xs_in_ref, xs_ref, sem):
    del xs_in_ref
    tc = x_ref.shape[0]

    def issue(r, carry):
        for k in range(TOP_K):
            _row_copy(x_ref, r, xs_ref, pos_ref[0, 0, r * TOP_K + k], sem).start()
        return carry

    lax.fori_loop(0, tc, issue, 0)

    def drain(r, carry):
        for k in range(TOP_K):
            _row_copy(x_ref, 0, xs_ref, 0, sem).wait()
        return carry

    lax.fori_loop(0, tc, drain, 0)


def _dispatch(pos, x1, n_slots, tile=256):
    n, d = x1.shape
    pos3 = pos.reshape(n // tile, 1, tile * TOP_K)
    xs0 = jnp.zeros((n_slots, d), F32)
    return pl.pallas_call(
        _dispatch_kernel,
        grid=(n // tile,),
        in_specs=[pl.BlockSpec((1, 1, tile * TOP_K), lambda i: (i, 0, 0), memory_space=pltpu.SMEM),
                  pl.BlockSpec((tile, d), lambda i: (i, 0)),
                  pl.BlockSpec(memory_space=pl.ANY)],
        out_specs=pl.BlockSpec(memory_space=pl.ANY),
        out_shape=jax.ShapeDtypeStruct((n_slots, d), F32),
        scratch_shapes=[pltpu.SemaphoreType.DMA(())],
        input_output_aliases={2: 0},
        compiler_params=_cparams("arbitrary"),
        name="moe_dispatch",
    )(pos3, x1, xs0)


def _expert_up_kernel(te_ref, nu_ref, x_ref, wg_ref, wu_ref, bg_ref, bu_ref, o_ref):
    @pl.when(pl.program_id(1) < nu_ref[0])
    def _():
        xb = x_ref[...].astype(BF16)
        g = _dot(xb, wg_ref[...]) + bg_ref[...]
        u = _dot(xb, wu_ref[...]) + bu_ref[...]
        g = jnp.minimum(g, SWIGLU_LIMIT)
        u = jnp.clip(u, -SWIGLU_LIMIT, SWIGLU_LIMIT)
        o_ref[...] = ((u + 1.0) * g * _sigmoid(SWIGLU_ALPHA * g)).astype(o_ref.dtype)


def _expert_up(te, nu, xs, w_gu, b_gu, splits=2):
    p, d = xs.shape
    ff = w_gu.shape[2] // 2
    fj = ff // splits
    tm = MOE_TM
    tile_of = lambda i, nu: jnp.minimum(i, nu[0] - 1)
    grid_spec = pltpu.PrefetchScalarGridSpec(
        num_scalar_prefetch=2,
        grid=(splits, p // tm),
        in_specs=[pl.BlockSpec((tm, d), lambda j, i, te, nu: (tile_of(i, nu), 0)),
                  pl.BlockSpec((None, d, fj), lambda j, i, te, nu: (te[i], 0, j)),
                  pl.BlockSpec((None, d, fj), lambda j, i, te, nu: (te[i], 0, splits + j)),
                  pl.BlockSpec((None, 1, fj), lambda j, i, te, nu: (te[i], 0, j)),
                  pl.BlockSpec((None, 1, fj), lambda j, i, te, nu: (te[i], 0, splits + j))],
        out_specs=pl.BlockSpec((tm, fj), lambda j, i, te, nu: (tile_of(i, nu), j)),
    )
    return pl.pallas_call(
        _expert_up_kernel,
        grid_spec=grid_spec,
        out_shape=jax.ShapeDtypeStruct((p, ff), BF16),
        compiler_params=_cparams("arbitrary", "arbitrary"),
        name="expert_up",
    )(te, nu, xs, w_gu, w_gu, b_gu, b_gu)


def _expert_down_kernel(te_ref, nu_ref, a_ref, wd_ref, bd_ref, o_ref):
    @pl.when(pl.program_id(0) < nu_ref[0])
    def _():
        o_ref[...] = _dot(a_ref[...], wd_ref[...]) + bd_ref[...]


def _expert_down(te, nu, act, w_down, b_down):
    p, ff = act.shape
    d = w_down.shape[2]
    tm = MOE_TM
    tile_of = lambda i, nu: jnp.minimum(i, nu[0] - 1)
    grid_spec = pltpu.PrefetchScalarGridSpec(
        num_scalar_prefetch=2,
        grid=(p // tm,),
        in_specs=[pl.BlockSpec((tm, ff), lambda i, te, nu: (tile_of(i, nu), 0)),
                  pl.BlockSpec((None, ff, d), lambda i, te, nu: (te[i], 0, 0)),
                  pl.BlockSpec((None, 1, d), lambda i, te, nu: (te[i], 0, 0))],
        out_specs=pl.BlockSpec((tm, d), lambda i, te, nu: (tile_of(i, nu), 0)),
    )
    return pl.pallas_call(
        _expert_down_kernel,
        grid_spec=grid_spec,
        out_shape=jax.ShapeDtypeStruct((p, d), F32),
        compiler_params=_cparams("arbitrary"),
        name="expert_down",
    )(te, nu, act, w_down, b_down)


def _combine_kernel(pos_ref, x_ref, gt_ref, g_ref, b_ref, eo_ref, o_ref, ob_ref, buf_ref, sem, *, alpha):
    tc = x_ref.shape[0]

    def issue(r, carry):
        for k in range(TOP_K):
            _row_copy(eo_ref, pos_ref[0, 0, r * TOP_K + k], buf_ref.at[k], r, sem).start()
        return carry

    lax.fori_loop(0, tc, issue, 0)

    def drain(r, carry):
        for k in range(TOP_K):
            _row_copy(eo_ref, 0, buf_ref.at[k], 0, sem).wait()
        return carry

    lax.fori_loop(0, tc, drain, 0)

    gt = gt_ref[...]
    y = alpha * x_ref[...]
    for k in range(TOP_K):
        y = y + gt[:, k:k + 1] * buf_ref[k]
    mu = jnp.mean(y, axis=-1, keepdims=True)
    yc = y - mu
    var = jnp.mean(yc * yc, axis=-1, keepdims=True)
    out = yc * lax.rsqrt(var + 1e-5) * g_ref[...] + b_ref[...]
    o_ref[...] = out
    ob_ref[...] = out.astype(ob_ref.dtype)


def _combine_ln(pos, x1, gates, g, b, eo, alpha, tile=256):
    n, d = x1.shape
    pos3 = pos.reshape(n // tile, 1, tile * TOP_K)
    row = lambda w: pl.BlockSpec((1, w), lambda i: (0, 0))
    return pl.pallas_call(
        functools.partial(_combine_kernel, alpha=alpha),
        grid=(n // tile,),
        in_specs=[pl.BlockSpec((1, 1, tile * TOP_K), lambda i: (i, 0, 0), memory_space=pltpu.SMEM),
                  pl.BlockSpec((tile, d), lambda i: (i, 0)),
                  pl.BlockSpec((tile, LANES), lambda i: (i, 0)),
                  row(d), row(d),
                  pl.BlockSpec(memory_space=pl.ANY)],
        out_specs=[pl.BlockSpec((tile, d), lambda i: (i, 0)),
                   pl.BlockSpec((tile, d), lambda i: (i, 0))],
        out_shape=[jax.ShapeDtypeStruct((n, d), F32), jax.ShapeDtypeStruct((n, d), BF16)],
        scratch_shapes=[pltpu.VMEM((TOP_K, tile, d), F32), pltpu.SemaphoreType.DMA(())],
        compiler_params=_cparams("arbitrary"),
        name="moe_combine_ln",
    )(pos3, x1, gates, g, b, eo)


def _slot_positions(top_i, n_tiles):
    tm = MOE_TM
    e_flat = top_i.reshape(-1)
    onehot = (e_flat[:, None] == jnp.arange(N_EXPERTS, dtype=I32)[None, :]).astype(I32)
    csum = jnp.cumsum(onehot, axis=0)
    rank = jnp.sum(csum * onehot, axis=1) - 1
    counts = csum[-1]
    padded = ((counts + tm - 1) // tm) * tm
    ends = jnp.cumsum(padded)
    pos = (ends - padded)[e_flat] + rank
    n_used = (ends[-1] // tm).astype(I32)
    tile_e = jnp.searchsorted(ends, jnp.arange(n_tiles, dtype=I32) * tm, side="right").astype(I32)
    last_e = jnp.max(jnp.where(counts > 0, jnp.arange(N_EXPERTS, dtype=I32), 0))
    tile_e = jnp.minimum(tile_e, last_e)
    return pos.astype(I32), tile_e, n_used.reshape(1)


def _in_proj_columns():
    gw = GROUP_WIDTH
    gm, ssm = 0, 2 * gw
    xbc_w = gw + 4 * SSM_STATE
    ml = ssm + gw + xbc_w + SSM_HEADS
    att = ml + 4 * gw + 2 * ML_HEADS
    r = lambda a, b: list(range(a, b))
    big = (r(gm, gm + 2 * gw) + r(ssm + gw, ssm + gw + xbc_w) + r(ssm, ssm + gw) + r(ml, ml + 4 * gw)
           + r(att + Q_RANK, att + Q_RANK + 2 * gw) + r(att, att + Q_RANK))
    kidx0 = att + Q_RANK + 2 * gw
    small = {0: r(kidx0, kidx0 + IDX_DIM + IDX_HEADS),
             LANES: r(ssm + gw + xbc_w, ssm + gw + xbc_w + SSM_HEADS),
             2 * LANES: r(ml + 4 * gw, ml + 4 * gw + 2 * ML_HEADS)}
    return np.asarray(big, np.int32), small


def _pad_row(v, width, fill=0.0):
    v = v.reshape(1, -1).astype(F32)
    return jnp.pad(v, ((0, 0), (0, width - v.shape[1])), constant_values=fill)


def _rope_rows():
    def rows(rot, starts):
        half = rot // 2
        inv = ROPE_THETA ** (-jnp.arange(half, dtype=F32) * 2.0 / rot)
        f = jnp.zeros((LANES,), F32)
        s = jnp.zeros((LANES,), F32)
        for st in starts:
            f = f.at[st:st + half].set(inv).at[st + half:st + rot].set(inv)
            s = s.at[st:st + half].set(-1.0).at[st + half:st + rot].set(1.0)
        return f.reshape(1, LANES), s.reshape(1, LANES)
    fm, sm = rows(ATT_DIM // 4, [0])
    fi, si = rows(IDX_DIM // 4, [0, IDX_DIM])
    fk, sk = rows(IDX_DIM // 4, [0])
    return fm, sm, fi, si, fk, sk


def kernel(x, positions, w_in, gm_ln_g, gm_ln_b, gm_ws, gm_bs, ssm_conv_w, ssm_conv_b, ssm_dt_bias, ssm_a_log,
           ssm_d, ssm_norm_g, ml_b_i, ml_b_f, ml_norm_g, att_q_norm_g, att_w_uq, idx_k_ln_g, idx_k_ln_b, w_out,
           ln1_g, ln1_b, ln2_g, ln2_b, router_w, router_b, expert_w_gu, expert_b_gu, expert_w_down,
           expert_b_down):
    bsz, seq, d = x.shape
    depth = w_in.shape[0]
    n = bsz * seq
    alpha = (2.0 * depth) ** 0.25
    n_slots = n * TOP_K + N_EXPERTS * MOE_TM
    n_tiles = n_slots // MOE_TM
    big_cols, small_cols = _in_proj_columns()
    pos = positions.reshape(n, 1).astype(I32)

    xf = x.reshape(n, d)
    xb = xf.astype(BF16)
    for l in range(depth):
        w_big = jnp.pad(jnp.take(w_in[l], big_cols, axis=1), ((0, 0), (0, H_COLS - big_cols.size))).astype(BF16)
        w_small = jnp.zeros((d, HS_COLS), F32)
        for c0, cols in small_cols.items():
            w_small = w_small.at[:, c0:c0 + len(cols)].set(jnp.take(w_in[l], np.asarray(cols, np.int32), axis=1))
        h = _matmul(xb, w_big, BF16, 1024, 1024)
        hs = _matmul(xb, w_small.astype(BF16), F32, 1024, HS_COLS)

        y_gm = _gmlp(h, gm_ln_g[l].reshape(1, -1), gm_ln_b[l].reshape(1, -1), gm_ws[l],
                     gm_bs[l].reshape(GM_HEADS, GM_BLOCK, 1))
        y_ssm = _ssd(h, hs, bsz, seq, ssm_conv_w[l], ssm_conv_b[l].reshape(1, -1),
                     _pad_row(ssm_dt_bias[l], LANES), _pad_row(-jnp.exp(ssm_a_log[l].astype(F32)), LANES),
                     jnp.repeat(ssm_d[l].astype(F32), SSM_HEADDIM).reshape(1, -1), ssm_norm_g[l].reshape(1, -1))
        y_ml = _mlstm(h, hs, bsz, seq, _pad_row(jnp.concatenate([ml_b_i[l], ml_b_f[l]]), LANES),
                      ml_norm_g[l].reshape(1, -1))
        y_att = _dsa(h, hs, pos, bsz, seq, att_q_norm_g[l], att_w_uq[l], idx_k_ln_g[l], idx_k_ln_b[l])

        x1, top_i, gates = _mix_ln_router(
            (y_gm, y_ssm, y_ml, y_att), xf, w_out[l].astype(BF16), ln1_g[l].reshape(1, -1), ln1_b[l].reshape(1, -1),
            jnp.pad(router_w[l], ((0, 0), (0, LANES - N_EXPERTS))),
            _pad_row(router_b[l], LANES, -jnp.inf), alpha)

        slot, tile_e, n_used = _slot_positions(top_i[:, :TOP_K], n_tiles)
        xs = _dispatch(slot, x1, n_slots)
        act = _expert_up(tile_e, n_used, xs, expert_w_gu[l].astype(BF16), expert_b_gu[l][:, None, :])
        eo = _expert_down(tile_e, n_used, act, expert_w_down[l].astype(BF16), expert_b_down[l][:, None, :])
        xf, xb = _combine_ln(slot, x1, gates, ln2_g[l].reshape(1, -1), ln2_b[l].reshape(1, -1), eo, alpha)
    return xf.reshape(bsz, seq, d)
```

```python
import functools
import math

import numpy as np
import jax
import jax.numpy as jnp
from jax import lax
from jax.experimental import pallas as pl
from jax.experimental.pallas import tpu as pltpu

F32 = jnp.float32
BF16 = jnp.bfloat16
I32 = jnp.int32

LANES = 128
CHUNK = 64
GROUP_WIDTH = 512
GM_BLOCK = 128
GM_HEADS = 4
SSM_HEADS = 8
SSM_HEADDIM = 64
SSM_STATE = 128
SSM_CONV = 4
ML_HEADS = 4
ML_DIM = 128
ATT_HEADS = 4
ATT_DIM = 128
Q_RANK = 384
IDX_HEADS = 8
IDX_DIM = 64
IDX_TOPK_MAX = 256
ROPE_THETA = 500000.0
N_EXPERTS = 32
TOP_K = 4
SWIGLU_LIMIT = 7.0
SWIGLU_ALPHA = 1.702

SEQ_CHUNK = 128
SEQ_TILE = 512
DSA_SB = 128
DSA_QB = 512
DSA_TK = 512
DSA_TILE_GROUP = 4
MOE_TM = 512
VMEM_LIMIT = 56 * 1024 * 1024

NEG_BIG = -1e30
INT_MIN = -(2 ** 31)
INT_MAX = 2 ** 31 - 1
NEG_INF_KEY = int(np.int32(np.uint32(0xFF800000) ^ np.uint32(0x7FFFFFFF)))

H_GM, H_XBC, H_Z, H_Q, H_K, H_V, H_O, H_AK, H_AV, H_CQ, H_COLS = (
    0, 1024, 2048, 2560, 3072, 3584, 4096, 4608, 5120, 5632, 6144)
HS_COLS = 384


def _cparams(*sem):
    return pltpu.CompilerParams(dimension_semantics=sem, vmem_limit_bytes=VMEM_LIMIT)


def _dot(a, b):
    return jnp.dot(a, b, preferred_element_type=F32)


def _dot_nt(a, b):
    return lax.dot_general(a, b, (((1,), (1,)), ((), ())), preferred_element_type=F32)


def _dot_exact_lhs(tri, x):
    hi = x.astype(BF16)
    r1 = x - hi.astype(F32)
    mid = r1.astype(BF16)
    lo = (r1 - mid.astype(F32)).astype(BF16)
    return _dot(tri, hi) + _dot(tri, mid) + _dot(tri, lo)


def _lower_tri(n):
    r = lax.broadcasted_iota(I32, (n, n), 0)
    c = lax.broadcasted_iota(I32, (n, n), 1)
    return r >= c


def _sigmoid(x):
    return 1.0 / (1.0 + jnp.exp(-x))


def _softplus(x):
    return jnp.maximum(x, 0.0) + jnp.log(1.0 + jnp.exp(-jnp.abs(x)))


def _log_sigmoid(x):
    return -_softplus(-x)


def _mm_kernel(a_ref, b_ref, o_ref):
    o_ref[...] = _dot(a_ref[...], b_ref[...]).astype(o_ref.dtype)


def _matmul(a, b, out_dtype, tm, tn):
    m, k = a.shape
    n = b.shape[1]
    return pl.pallas_call(
        _mm_kernel,
        grid=(n // tn, m // tm),
        in_specs=[pl.BlockSpec((tm, k), lambda j, i: (i, 0)),
                  pl.BlockSpec((k, tn), lambda j, i: (0, j))],
        out_specs=pl.BlockSpec((tm, tn), lambda j, i: (i, j)),
        out_shape=jax.ShapeDtypeStruct((m, n), out_dtype),
        compiler_params=_cparams("parallel", "parallel"),
        name="in_proj",
    )(a, b)


def _gmlp_kernel(h_ref, lng_ref, lnb_ref, ws_ref, bs_ref, o_ref):
    t = h_ref.shape[0]
    h = h_ref[...].astype(F32)
    h = 0.5 * h * (1.0 + lax.erf(h * (1.0 / math.sqrt(2.0))))
    r = lax.broadcasted_iota(I32, (GM_BLOCK, GM_BLOCK), 0)
    c = lax.broadcasted_iota(I32, (GM_BLOCK, GM_BLOCK), 1)
    allowed = (c // CHUNK) <= (r // CHUNK)
    for g in range(GM_HEADS):
        u = h[:, g * LANES:(g + 1) * LANES]
        v = h[:, GROUP_WIDTH + g * LANES:GROUP_WIDTH + (g + 1) * LANES]
        mu = jnp.mean(v, axis=-1, keepdims=True)
        vc = v - mu
        var = jnp.mean(vc * vc, axis=-1, keepdims=True)
        vn = vc * lax.rsqrt(var + 1e-5) * lng_ref[:, g * LANES:(g + 1) * LANES] \
            + lnb_ref[:, g * LANES:(g + 1) * LANES]
        w = jnp.where(allowed, ws_ref[g], 0.0).astype(BF16)
        for wdw in range(t // GM_BLOCK):
            rows = slice(wdw * GM_BLOCK, (wdw + 1) * GM_BLOCK)
            vmix = _dot(w, vn[rows].astype(BF16)) + bs_ref[g]
            o_ref[rows, g * LANES:(g + 1) * LANES] = (u[rows] * vmix).astype(o_ref.dtype)


def _gmlp(h, lng, lnb, ws, bs, tile=256):
    n = h.shape[0]
    return pl.pallas_call(
        _gmlp_kernel,
        grid=(n // tile,),
        in_specs=[pl.BlockSpec((tile, 2 * GROUP_WIDTH), lambda i: (i, H_GM // (2 * GROUP_WIDTH))),
                  pl.BlockSpec((1, GROUP_WIDTH), lambda i: (0, 0)),
                  pl.BlockSpec((1, GROUP_WIDTH), lambda i: (0, 0)),
                  pl.BlockSpec((GM_HEADS, GM_BLOCK, GM_BLOCK), lambda i: (0, 0, 0)),
                  pl.BlockSpec((GM_HEADS, GM_BLOCK, 1), lambda i: (0, 0, 0))],
        out_specs=pl.BlockSpec((tile, GROUP_WIDTH), lambda i: (i, 0)),
        out_shape=jax.ShapeDtypeStruct((n, GROUP_WIDTH), BF16),
        compiler_params=_cparams("parallel"),
        name="gmlp",
    )(h, lng, lnb, ws, bs)


def _ssd_kernel(xbc_ref, z_ref, dt_ref, cw_ref, cb_ref, dtb_ref, a_ref, dsk_ref, ng_ref, o_ref,
                buf_ref, act_ref, st_ref):
    t = xbc_ref.shape[0]
    lc = SEQ_CHUNK

    @pl.when(pl.program_id(1) == 0)
    def _():
        buf_ref[0:8, :] = jnp.zeros((8, buf_ref.shape[1]), F32)
        st_ref[...] = jnp.zeros(st_ref.shape, F32)

    buf_ref[8:8 + t, :] = xbc_ref[...].astype(F32)
    conv = cb_ref[...] + cw_ref[0:1, :] * buf_ref[5:5 + t, :]
    for k in range(1, SSM_CONV):
        conv = conv + cw_ref[k:k + 1, :] * buf_ref[5 + k:5 + k + t, :]
    act_ref[...] = conv * _sigmoid(conv)
    buf_ref[0:8, :] = buf_ref[t:t + 8, :]

    tri = _lower_tri(lc)
    tri_b = tri.astype(BF16)
    lane = lax.broadcasted_iota(I32, (lc, LANES), 1)
    first_half = lane < SSM_HEADDIM
    a_row = a_ref[...]

    def chunk(ci, carry):
        r0 = pl.multiple_of(ci * lc, lc)
        xa = act_ref[pl.ds(r0, lc), :]
        dt = _softplus(dt_ref[pl.ds(r0, lc), :] + dtb_ref[...])
        acs = _dot_exact_lhs(tri_b, dt * a_row)
        acs_t = acs.T
        ys = []
        for p in range(SSM_HEADS // 2):
            g = p // 2
            h0, h1 = 2 * p, 2 * p + 1
            bm = xa[:, GROUP_WIDTH + g * SSM_STATE:GROUP_WIDTH + (g + 1) * SSM_STATE]
            cm = xa[:, GROUP_WIDTH + 2 * SSM_STATE + g * SSM_STATE:GROUP_WIDTH + 2 * SSM_STATE + (g + 1) * SSM_STATE]
            cm_b = cm.astype(BF16)
            cb = _dot_nt(cm_b, bm.astype(BF16))
            x2 = xa[:, p * LANES:(p + 1) * LANES]
            dt2 = jnp.where(first_half, dt[:, h0:h0 + 1], dt[:, h1:h1 + 1])
            xdt = x2 * dt2
            xdt_b = xdt.astype(BF16)
            yd = []
            for hh in (h0, h1):
                decay = jnp.where(tri, jnp.exp(acs[:, hh:hh + 1] - acs_t[hh:hh + 1, :]), 0.0)
                yd.append(_dot((cb * decay).astype(BF16), xdt_b))
            y_diag = jnp.where(first_half, yd[0], yd[1])
            acs2 = jnp.where(first_half, acs[:, h0:h0 + 1], acs[:, h1:h1 + 1])
            aend2 = jnp.where(first_half[0:1], acs[lc - 1:lc, h0:h0 + 1], acs[lc - 1:lc, h1:h1 + 1])
            st = st_ref[p]
            y_off = _dot(cm_b, st.astype(BF16)) * jnp.exp(acs2)
            ys.append(y_diag + y_off + x2 * dsk_ref[:, p * LANES:(p + 1) * LANES])
            upd = _dot(bm.T.astype(BF16), (xdt * jnp.exp(aend2 - acs2)).astype(BF16))
            st_ref[p] = jnp.exp(aend2) * st + upd
        y = jnp.concatenate(ys, axis=-1)
        zz = z_ref[pl.ds(r0, lc), :].astype(F32)
        y = y * (zz * _sigmoid(zz))
        y = y * lax.rsqrt(jnp.mean(y * y, axis=-1, keepdims=True) + 1e-6) * ng_ref[...]
        o_ref[pl.ds(r0, lc), :] = y.astype(o_ref.dtype)
        return carry

    lax.fori_loop(0, t // lc, chunk, 0)


def _ssd(h, hs, bsz, seq, cw, cb, dtb, a_row, dsk, ng, tile=SEQ_TILE):
    n = h.shape[0]
    nt = seq // tile
    xw = 2 * GROUP_WIDTH
    return pl.pallas_call(
        _ssd_kernel,
        grid=(bsz, nt),
        in_specs=[pl.BlockSpec((tile, xw), lambda b, j: (b * nt + j, H_XBC // xw)),
                  pl.BlockSpec((tile, GROUP_WIDTH), lambda b, j: (b * nt + j, H_Z // GROUP_WIDTH)),
                  pl.BlockSpec((tile, LANES), lambda b, j: (b * nt + j, 1)),
                  pl.BlockSpec((SSM_CONV, xw), lambda b, j: (0, 0)),
                  pl.BlockSpec((1, xw), lambda b, j: (0, 0)),
                  pl.BlockSpec((1, LANES), lambda b, j: (0, 0)),
                  pl.BlockSpec((1, LANES), lambda b, j: (0, 0)),
                  pl.BlockSpec((1, GROUP_WIDTH), lambda b, j: (0, 0)),
                  pl.BlockSpec((1, GROUP_WIDTH), lambda b, j: (0, 0))],
        out_specs=pl.BlockSpec((tile, GROUP_WIDTH), lambda b, j: (b * nt + j, 0)),
        out_shape=jax.ShapeDtypeStruct((n, GROUP_WIDTH), BF16),
        scratch_shapes=[pltpu.VMEM((tile + 8, xw), F32),
                        pltpu.VMEM((tile, xw), F32),
                        pltpu.VMEM((SSM_HEADS // 2, SSM_STATE, LANES), F32)],
        compiler_params=_cparams("arbitrary", "arbitrary"),
        name="ssd",
    )(h, h, hs, cw, cb, dtb, a_row, dsk, ng)


def _mlstm_kernel(q_ref, k_ref, v_ref, og_ref, gt_ref, bias_ref, ng_ref, o_ref, c_ref, n_ref, m_ref):
    t = q_ref.shape[0]
    lc = SEQ_CHUNK
    scale = ML_DIM ** -0.5

    @pl.when(pl.program_id(1) == 0)
    def _():
        c_ref[...] = jnp.zeros(c_ref.shape, F32)
        n_ref[...] = jnp.zeros(n_ref.shape, F32)
        m_ref[...] = jnp.zeros(m_ref.shape, F32)

    tri = _lower_tri(lc)
    tri_b = tri.astype(BF16)

    def chunk(ci, carry):
        r0 = pl.multiple_of(ci * lc, lc)
        gts = gt_ref[pl.ds(r0, lc), :] + bias_ref[...]
        bcum = _dot_exact_lhs(tri_b, _log_sigmoid(gts))
        gts_t = gts.T
        bcum_t = bcum.T
        for hd in range(ML_HEADS):
            cols = slice(hd * ML_DIM, (hd + 1) * ML_DIM)
            qh = q_ref[pl.ds(r0, lc), cols]
            kh = (k_ref[pl.ds(r0, lc), cols].astype(F32) * scale).astype(BF16)
            vh = v_ref[pl.ds(r0, lc), cols]
            i_col = gts[:, hd:hd + 1]
            i_row = gts_t[hd:hd + 1, :]
            b_col = bcum[:, ML_HEADS + hd:ML_HEADS + hd + 1]
            b_row = bcum_t[ML_HEADS + hd:ML_HEADS + hd + 1, :]
            b_end = b_col[lc - 1:lc, :]
            m_prev = m_ref[hd][0:1, 0:1]
            c_prev = c_ref[hd]
            n_prev = n_ref[hd][0:1, :]
            log_d = jnp.where(tri, b_col - b_row + i_row, -jnp.inf)
            inter = b_col + m_prev
            m_j = jnp.maximum(jnp.max(log_d, axis=-1, keepdims=True), inter)
            s_mat = jnp.exp(log_d - m_j) * _dot_nt(qh, kh)
            gsc = jnp.exp(inter - m_j)
            num = _dot(s_mat.astype(BF16), vh) + gsc * _dot_nt(qh, c_prev.astype(BF16))
            den = jnp.sum(s_mat, axis=-1, keepdims=True) \
                + gsc * jnp.sum(qh.astype(F32) * n_prev, axis=-1, keepdims=True)
            hh = num / jnp.maximum(jnp.abs(den), jnp.exp(-m_j))
            mu = jnp.mean(hh, axis=-1, keepdims=True)
            hc = hh - mu
            var = jnp.mean(hc * hc, axis=-1, keepdims=True)
            hn = hc * lax.rsqrt(var + 1e-5) * ng_ref[:, cols]
            og = og_ref[pl.ds(r0, lc), cols].astype(F32)
            o_ref[pl.ds(r0, lc), cols] = (hn * _sigmoid(og)).astype(o_ref.dtype)
            a_col = b_end - b_col + i_col
            a_row = b_end - b_row + i_row
            m_new = jnp.maximum(b_end + m_prev, jnp.max(a_row, axis=-1, keepdims=True))
            decay = jnp.exp(b_end + m_prev - m_new)
            w_col = jnp.exp(a_col - m_new)
            kf = kh.astype(F32)
            vw_t = (vh.astype(F32) * w_col).T.astype(BF16)
            c_ref[hd] = decay * c_prev + _dot(vw_t, kh)
            n_new = decay * n_prev + jnp.sum(w_col * kf, axis=0, keepdims=True)
            n_ref[hd] = jnp.broadcast_to(n_new, (8, ML_DIM))
            m_ref[hd] = jnp.broadcast_to(m_new, (8, LANES))
        return carry

    lax.fori_loop(0, t // lc, chunk, 0)


def _mlstm(h, hs, bsz, seq, bias, ng, tile=SEQ_TILE):
    n = h.shape[0]
    nt = seq // tile
    gw = GROUP_WIDTH

    def col(c0):
        return pl.BlockSpec((tile, gw), lambda b, j: (b * nt + j, c0 // gw))

    return pl.pallas_call(
        _mlstm_kernel,
        grid=(bsz, nt),
        in_specs=[col(H_Q), col(H_K), col(H_V), col(H_O),
                  pl.BlockSpec((tile, LANES), lambda b, j: (b * nt + j, 2)),
                  pl.BlockSpec((1, LANES), lambda b, j: (0, 0)),
                  pl.BlockSpec((1, gw), lambda b, j: (0, 0))],
        out_specs=pl.BlockSpec((tile, gw), lambda b, j: (b * nt + j, 0)),
        out_shape=jax.ShapeDtypeStruct((n, gw), BF16),
        scratch_shapes=[pltpu.VMEM((ML_HEADS, ML_DIM, ML_DIM), F32),
                        pltpu.VMEM((ML_HEADS, 8, ML_DIM), F32),
                        pltpu.VMEM((ML_HEADS, 8, LANES), F32)],
        compiler_params=_cparams("arbitrary", "arbitrary"),
        name="mlstm",
    )(h, h, h, h, hs, bias, ng)


def _att_prep_kernel(cq_ref, ak_ref, hs_ref, pos_ref, qg_ref, wuq_ref, kg_ref, kb_ref,
                     fm_ref, sm_ref, fi_ref, si_ref, fk_ref, sk_ref,
                     q_ref, k_ref, qi_ref, ki_ref, w_ref):
    t = cq_ref.shape[0]
    lane = lax.broadcasted_iota(I32, (t, LANES), 1)
    pos = pos_ref[...].astype(F32)

    cq = cq_ref[:, 0:Q_RANK].astype(F32)
    cq = cq * lax.rsqrt(jnp.mean(cq * cq, axis=-1, keepdims=True) + 1e-6) * qg_ref[...]
    q_all = _dot(cq.astype(BF16), wuq_ref[...])

    def rope(x, cos, sin_signed, half, period):
        first = (lane % period) < half
        partner = jnp.where(first, pltpu.roll(x, LANES - half, 1), pltpu.roll(x, half, 1))
        return x * cos + partner * sin_signed

    ang = pos * fm_ref[...]
    cos_m, sin_m = jnp.cos(ang), jnp.sin(ang) * sm_ref[...]
    half_m = ATT_DIM // 8
    for hd in range(ATT_HEADS):
        cols = slice(hd * ATT_DIM, (hd + 1) * ATT_DIM)
        q_ref[:, cols] = (rope(q_all[:, cols], cos_m, sin_m, half_m, LANES) * (ATT_DIM ** -0.5)).astype(q_ref.dtype)
        k_ref[:, cols] = rope(ak_ref[:, cols].astype(F32), cos_m, sin_m, half_m, LANES).astype(k_ref.dtype)

    ang = pos * fi_ref[...]
    cos_i, sin_i = jnp.cos(ang), jnp.sin(ang) * si_ref[...]
    half_i = IDX_DIM // 8
    low = lane < IDX_DIM
    for pr in range(IDX_HEADS // 2):
        x = rope(q_all[:, GROUP_WIDTH + pr * LANES:GROUP_WIDTH + (pr + 1) * LANES], cos_i, sin_i, half_i, IDX_DIM)
        even = jnp.where(low, x, 0.0).astype(qi_ref.dtype)
        odd = jnp.where(low, pltpu.roll(x, IDX_DIM, 1), 0.0).astype(qi_ref.dtype)
        for blk in range(t // DSA_SB):
            rows = slice(blk * DSA_SB, (blk + 1) * DSA_SB)
            qi_ref[blk, 2 * pr] = even[rows]
            qi_ref[blk, 2 * pr + 1] = odd[rows]

    hs = hs_ref[...]
    kx = jnp.where(low, hs, 0.0)
    mu = jnp.sum(kx, axis=-1, keepdims=True) * (1.0 / IDX_DIM)
    kc = jnp.where(low, hs - mu, 0.0)
    var = jnp.sum(kc * kc, axis=-1, keepdims=True) * (1.0 / IDX_DIM)
    kn = kc * lax.rsqrt(var + 1e-5) * kg_ref[...] + kb_ref[...]
    ang = pos * fk_ref[...]
    ki_ref[...] = rope(kn, jnp.cos(ang), jnp.sin(ang) * sk_ref[...], half_i, LANES).astype(ki_ref.dtype)
    wi = jnp.where(lane < IDX_HEADS, pltpu.roll(hs, LANES - IDX_DIM, 1), 0.0) * (IDX_HEADS ** -0.5 * IDX_DIM ** -0.5)
    for blk in range(t // DSA_SB):
        w_ref[blk] = wi[blk * DSA_SB:(blk + 1) * DSA_SB, :].T[0:IDX_HEADS, :]


def _att_prep(h, hs, pos, qg, wuq, kg, kb, rope_rows, tile=256):
    n = h.shape[0]
    gw = GROUP_WIDTH
    row = lambda w: pl.BlockSpec((1, w), lambda i: (0, 0))
    return pl.pallas_call(
        _att_prep_kernel,
        grid=(n // tile,),
        in_specs=[pl.BlockSpec((tile, gw), lambda i: (i, H_CQ // gw)),
                  pl.BlockSpec((tile, gw), lambda i: (i, H_AK // gw)),
                  pl.BlockSpec((tile, LANES), lambda i: (i, 0)),
                  pl.BlockSpec((tile, 1), lambda i: (i, 0)),
                  row(Q_RANK),
                  pl.BlockSpec(wuq.shape, lambda i: (0, 0)),
                  row(LANES), row(LANES)] + [row(LANES)] * 6,
        out_specs=[pl.BlockSpec((tile, gw), lambda i: (i, 0)),
                   pl.BlockSpec((tile, gw), lambda i: (i, 0)),
                   pl.BlockSpec((tile // DSA_SB, IDX_HEADS, DSA_SB, LANES), lambda i: (i, 0, 0, 0)),
                   pl.BlockSpec((tile, LANES), lambda i: (i, 0)),
                   pl.BlockSpec((tile // DSA_SB, IDX_HEADS, DSA_SB), lambda i: (i, 0, 0))],
        out_shape=[jax.ShapeDtypeStruct((n, gw), BF16),
                   jax.ShapeDtypeStruct((n, gw), BF16),
                   jax.ShapeDtypeStruct((n // DSA_SB, IDX_HEADS, DSA_SB, LANES), BF16),
                   jax.ShapeDtypeStruct((n, LANES), BF16),
                   jax.ShapeDtypeStruct((n // DSA_SB, IDX_HEADS, DSA_SB), F32)],
        compiler_params=_cparams("parallel"),
        name="att_prep",
    )(h, h, hs, pos, qg, wuq, kg, kb, *rope_rows)


def _dsa_select_kernel(qi_ref, w_ref, ki_ref, bias_ref, key_ref, hi_ref, lo_ref, *, top_k, idx_bits):
    sb = w_ref.shape[2]
    tk = key_ref.shape[1]
    nt_all = bias_ref.shape[1]
    i = pl.program_id(1)
    n_tiles = (i * sb + sb + tk - 1) // tk
    row = lax.broadcasted_iota(I32, (1, sb), 1)
    lim = ((i * sb + row) // CHUNK + 1) * CHUNK
    sub_t = lax.broadcasted_iota(I32, (tk, sb), 0)
    w = w_ref[0]
    half_min = -(2 ** 15)

    def fill(kt, carry):
        k0 = pl.multiple_of(kt * tk, tk)
        kk = ki_ref[pl.ds(k0, tk), :]
        acc = None
        for pr in range(IDX_HEADS // 2):
            rel = _dot_nt(kk, qi_ref[2 * pr * sb:(2 * pr + 2) * sb, :])
            for u in range(2):
                term = jnp.maximum(rel[:, u * sb:(u + 1) * sb], 0.0) * w[2 * pr + u:2 * pr + u + 1, :]
                acc = term if acc is None else acc + term
        score = jnp.where(sub_t + k0 < lim, acc + 0.0, -jnp.inf)
        bits = pltpu.bitcast(score, I32)
        keys = bits ^ ((bits >> 31) & INT_MAX)
        key_ref[kt] = keys
        hi_ref[kt] = (keys >> 16).astype(jnp.int16)
        return carry

    lax.fori_loop(0, n_tiles, fill, 0)

    group = DSA_TILE_GROUP
    n_groups = (n_tiles + group - 1) // group

    def pad_tile(kt, carry):
        key_ref[kt] = jnp.full((tk, sb), NEG_INF_KEY, I32)
        hi_ref[kt] = jnp.full((tk, sb), NEG_INF_KEY >> 16, jnp.int16)
        return carry

    lax.fori_loop(n_tiles, n_groups * group, pad_tile, 0)

    def tree_sum(parts):
        while len(parts) > 1:
            parts = [parts[j] + parts[j + 1] for j in range(0, len(parts), 2)]
        return parts[0]

    def count(ref, pred, dtype):
        gran = 32 // jnp.dtype(dtype).itemsize

        def body(g, acc):
            for u in range(group):
                kt = g * group + u
                m = pred(ref[kt], kt).astype(dtype)
                acc = acc + tree_sum([m[c * gran:(c + 1) * gran] for c in range(tk // gran)])
            return acc
        acc = lax.fori_loop(0, n_groups, body, jnp.zeros((gran, sb), dtype))
        return jnp.sum(acc.astype(I32), axis=0, keepdims=True)

    def count16_ge(ref, cand):
        c16 = cand.astype(jnp.int16)
        return count(ref, lambda v, kt: v >= c16, jnp.int16)

    def greedy16(ref, need):
        def step(s, t):
            cand = t + lax.shift_left(jnp.int32(1), 15 - s)
            return jnp.where(count16_ge(ref, cand) >= need, cand, t)
        return lax.fori_loop(0, 16, step, jnp.full((1, sb), half_min, I32))

    t_hi = greedy16(hi_ref, top_k)
    t_hi16 = t_hi.astype(jnp.int16)
    above = count(hi_ref, lambda v, kt: v > t_hi16, jnp.int16)
    tied_hi = count(hi_ref, lambda v, kt: v == t_hi16, jnp.int16)
    need_lo = top_k - above

    def fill_lo(kt, carry):
        lo = ((key_ref[kt] & 0xFFFF) + half_min).astype(jnp.int16)
        lo_ref[kt] = jnp.where(hi_ref[kt] == t_hi16, lo, jnp.int16(half_min))
        return carry

    lax.fori_loop(0, n_groups * group, fill_lo, 0)
    t_lo = greedy16(lo_ref, need_lo)
    thr = (t_hi << 16) | ((t_lo - half_min) & 0xFFFF)
    at_least = above + jnp.where(t_lo == half_min, tied_hi, count16_ge(lo_ref, t_lo))
    excess = (at_least > top_k) & (thr > NEG_INF_KEY)

    def emit(bound):
        def body(kt, carry):
            keys = key_ref[kt]
            sidx = sub_t + kt * tk
            sel = ((keys > thr) | ((keys == thr) & (sidx < bound))) & (sidx < lim)
            bias_t = jnp.where(sel, 0.0, -jnp.inf)
            for c in range(tk // LANES):
                bias_ref[0, kt, :, c * LANES:(c + 1) * LANES] = \
                    bias_t[c * LANES:(c + 1) * LANES, :].T.astype(bias_ref.dtype)
            return carry
        lax.fori_loop(0, n_tiles, body, 0)

    any_excess = jnp.max(excess.astype(I32)) > 0

    @pl.when(jnp.logical_not(any_excess))
    def _():
        emit(jnp.full((1, sb), INT_MAX, I32))

    @pl.when(any_excess)
    def _():
        need = top_k - count(key_ref, lambda v, kt: v > thr, I32)

        def idx_step(it, bound):
            cand = bound | lax.shift_left(jnp.int32(1), idx_bits - 1 - it)
            below = count(key_ref, lambda v, kt: (v == thr) & (sub_t + kt * tk < cand), I32)
            return jnp.where(below <= need, cand, bound)

        bound = lax.fori_loop(0, idx_bits, idx_step, jnp.zeros((1, sb), I32))
        emit(jnp.where(excess, bound, INT_MAX))

    def blank(kt, carry):
        bias_ref[0, kt] = jnp.full(bias_ref.shape[2:], -jnp.inf, bias_ref.dtype)
        return carry

    lax.fori_loop(n_tiles, nt_all, blank, 0)


def _dsa_select(qi, w, ki, bsz, seq, top_k):
    nb, _, sb = w.shape
    tk = DSA_TK
    nq, nt = seq // sb, seq // tk
    nt_pad = -(-nt // DSA_TILE_GROUP) * DSA_TILE_GROUP
    kern = functools.partial(_dsa_select_kernel, top_k=top_k, idx_bits=int(seq).bit_length())
    return pl.pallas_call(
        kern,
        grid=(bsz, nq),
        in_specs=[pl.BlockSpec((IDX_HEADS * sb, LANES), lambda b, i: (b * nq + i, 0)),
                  pl.BlockSpec((1, IDX_HEADS, sb), lambda b, i: (b * nq + i, 0, 0)),
                  pl.BlockSpec((seq, LANES), lambda b, i: (b, 0))],
        out_specs=pl.BlockSpec((1, nt, sb, tk), lambda b, i: (b * nq + i, 0, 0, 0)),
        out_shape=jax.ShapeDtypeStruct((nb, nt, sb, tk), BF16),
        scratch_shapes=[pltpu.VMEM((nt_pad, tk, sb), I32),
                        pltpu.VMEM((nt_pad, tk, sb), jnp.int16),
                        pltpu.VMEM((nt_pad, tk, sb), jnp.int16)],
        compiler_params=_cparams("parallel", "parallel"),
        name="dsa_select",
    )(qi.reshape(nb * IDX_HEADS * sb, LANES), w, ki)


def _dsa_attn_kernel(qt_ref, kt_ref, first_ref, last_ref, q_ref, bias_ref, k_ref, v_ref, o_ref,
                     m_ref, l_ref, acc_ref):
    s = pl.program_id(1)
    qb, tk = q_ref.shape[0], k_ref.shape[0]

    @pl.when(first_ref[s] == 1)
    def _():
        m_ref[...] = jnp.full(m_ref.shape, NEG_BIG, F32)
        l_ref[...] = jnp.zeros(l_ref.shape, F32)
        acc_ref[...] = jnp.zeros(acc_ref.shape, F32)

    bias = bias_ref[...].reshape(qb, tk).astype(F32)

    for hd in range(ATT_HEADS):
        cols = slice(hd * ATT_DIM, (hd + 1) * ATT_DIM)
        logits = _dot_nt(q_ref[:, cols], k_ref[:, cols]) + bias
        m_prev = m_ref[:, cols]
        m_new = jnp.maximum(m_prev, jnp.max(logits, axis=-1, keepdims=True))
        p = jnp.exp(logits - m_new[:, 0:1])
        alpha = jnp.exp(m_prev - m_new)
        l_ref[:, cols] = alpha * l_ref[:, cols] + jnp.sum(p, axis=-1, keepdims=True)
        acc_ref[:, cols] = alpha * acc_ref[:, cols] + _dot(p.astype(BF16), v_ref[:, cols])
        m_ref[:, cols] = m_new

    @pl.when(last_ref[s] == 1)
    def _():
        o_ref[...] = (acc_ref[...] / l_ref[...]).astype(o_ref.dtype)


def _dsa_attention(q, bias, k, h, bsz, seq):
    n = q.shape[0]
    qb, tk = DSA_QB, DSA_TK
    nq, nkt = seq // qb, seq // tk
    sub = qb // DSA_SB
    pairs = [(i, t) for i in range(nq) for t in range((i * qb + qb - 1) // tk + 1)]
    qt = jnp.asarray([p[0] for p in pairs], I32)
    ktab = jnp.asarray([p[1] for p in pairs], I32)
    first = jnp.asarray([1 if p[1] == 0 else 0 for p in pairs], I32)
    last = jnp.asarray([1 if p[1] == (p[0] * qb + qb - 1) // tk else 0 for p in pairs], I32)
    gw = GROUP_WIDTH

    def qmap(w_):
        return pl.BlockSpec((qb, w_), lambda b, s, qt, kt, f, l: (b * nq + qt[s], 0))

    def kmap(w_, cb):
        return pl.BlockSpec((tk, w_), lambda b, s, qt, kt, f, l: (b * nkt + kt[s], cb))

    grid_spec = pltpu.PrefetchScalarGridSpec(
        num_scalar_prefetch=4,
        grid=(bsz, len(pairs)),
        in_specs=[qmap(gw),
                  pl.BlockSpec((sub, 1, DSA_SB, tk), lambda b, s, qt, kt, f, l: (b * nq + qt[s], kt[s], 0, 0)),
                  kmap(gw, 0), kmap(gw, H_AV // gw)],
        out_specs=qmap(gw),
        scratch_shapes=[pltpu.VMEM((qb, gw), F32), pltpu.VMEM((qb, gw), F32), pltpu.VMEM((qb, gw), F32)],
    )
    return pl.pallas_call(
        _dsa_attn_kernel,
        grid_spec=grid_spec,
        out_shape=jax.ShapeDtypeStruct((n, gw), BF16),
        compiler_params=_cparams("arbitrary", "arbitrary"),
        name="dsa_attention",
    )(qt, ktab, first, last, q, bias, k, h)


def _dsa(h, hs, pos, bsz, seq, q_norm_g, w_uq, idxk_g, idxk_b):
    q, k, qi, ki, wi = _att_prep(h, hs, pos, q_norm_g.reshape(1, -1), w_uq.astype(BF16),
                                 _pad_row(idxk_g, LANES), _pad_row(idxk_b, LANES), _rope_rows())
    bias = _dsa_select(qi, wi, ki, bsz, seq, min(IDX_TOPK_MAX, seq // 4))
    return _dsa_attention(q, bias, k, h, bsz, seq)


def _mix_kernel(ya_ref, yb_ref, yc_ref, yd_ref, x_ref, wo_ref, g_ref, b_ref, rw_ref, rb_ref,
                x1_ref, ti_ref, gt_ref, *, alpha):
    gw = GROUP_WIDTH
    mix = _dot(ya_ref[...], wo_ref[0:gw, :])
    mix = mix + _dot(yb_ref[...], wo_ref[gw:2 * gw, :])
    mix = mix + _dot(yc_ref[...], wo_ref[2 * gw:3 * gw, :])
    mix = mix + _dot(yd_ref[...], wo_ref[3 * gw:4 * gw, :])
    y = alpha * x_ref[...] + mix
    mu = jnp.mean(y, axis=-1, keepdims=True)
    yc = y - mu
    var = jnp.mean(yc * yc, axis=-1, keepdims=True)
    x1 = yc * lax.rsqrt(var + 1e-5) * g_ref[...] + b_ref[...]
    x1_ref[...] = x1

    logits = jnp.dot(x1, rw_ref[...], precision=lax.Precision.HIGHEST, preferred_element_type=F32) + rb_ref[...]
    lane = lax.broadcasted_iota(I32, logits.shape, 1)
    vals, idxs = [], []
    for _ in range(TOP_K):
        mx = jnp.max(logits, axis=-1, keepdims=True)
        ix = jnp.min(jnp.where(logits == mx, lane, LANES), axis=-1, keepdims=True)
        vals.append(mx)
        idxs.append(ix)
        logits = jnp.where(lane == ix, -jnp.inf, logits)
    es = [jnp.exp(v - vals[0]) for v in vals]
    tot = es[0] + es[1] + es[2] + es[3]
    ti = jnp.zeros(logits.shape, I32)
    gt = jnp.zeros(logits.shape, F32)
    for k in range(TOP_K):
        ti = jnp.where(lane == k, idxs[k], ti)
        gt = jnp.where(lane == k, es[k] / tot, gt)
    ti_ref[...] = ti
    gt_ref[...] = gt


def _mix_ln_router(ys, x, wo, g, b, rw, rb, alpha, tile=256):
    n, d = x.shape
    gw = GROUP_WIDTH
    yspec = pl.BlockSpec((tile, gw), lambda i: (i, 0))
    row = lambda w: pl.BlockSpec((1, w), lambda i: (0, 0))
    return pl.pallas_call(
        functools.partial(_mix_kernel, alpha=alpha),
        grid=(n // tile,),
        in_specs=[yspec, yspec, yspec, yspec,
                  pl.BlockSpec((tile, d), lambda i: (i, 0)),
                  pl.BlockSpec(wo.shape, lambda i: (0, 0)),
                  row(d), row(d),
                  pl.BlockSpec(rw.shape, lambda i: (0, 0)),
                  row(LANES)],
        out_specs=[pl.BlockSpec((tile, d), lambda i: (i, 0)),
                   pl.BlockSpec((tile, LANES), lambda i: (i, 0)),
                   pl.BlockSpec((tile, LANES), lambda i: (i, 0))],
        out_shape=[jax.ShapeDtypeStruct((n, d), F32),
                   jax.ShapeDtypeStruct((n, LANES), I32),
                   jax.ShapeDtypeStruct((n, LANES), F32)],
        compiler_params=_cparams("parallel"),
        name="mix_ln_router",
    )(*ys, x, wo, g, b, rw, rb)


def _row_copy(src, s, dst, d, sem):
    return pltpu.make_async_copy(src.at[pl.ds(s, 1), :], dst.at[pl.ds(d, 1), :], sem)


def _dispatch_kernel(pos_ref, x_ref, xs_in_ref, xs_ref, sem):
    del xs_in_ref
    tc = x_ref.shape[0]

    def issue(r, carry):
        for k in range(TOP_K):
            _row_copy(x_ref, r, xs_ref, pos_ref[0, 0, r * TOP_K + k], sem).start()
        return carry

    lax.fori_loop(0, tc, issue, 0)

    def drain(r, carry):
        for k in range(TOP_K):
            _row_copy(x_ref, 0, xs_ref, 0, sem).wait()
        return carry

    lax.fori_loop(0, tc, drain, 0)


def _dispatch(pos, x1, n_slots, tile=256):
    n, d = x1.shape
    pos3 = pos.reshape(n // tile, 1, tile * TOP_K)
    xs0 = jnp.zeros((n_slots, d), F32)
    return pl.pallas_call(
        _dispatch_kernel,
        grid=(n // tile,),
        in_specs=[pl.BlockSpec((1, 1, tile * TOP_K), lambda i: (i, 0, 0), memory_space=pltpu.SMEM),
                  pl.BlockSpec((tile, d), lambda i: (i, 0)),
                  pl.BlockSpec(memory_space=pl.ANY)],
        out_specs=pl.BlockSpec(memory_space=pl.ANY),
        out_shape=jax.ShapeDtypeStruct((n_slots, d), F32),
        scratch_shapes=[pltpu.SemaphoreType.DMA(())],
        input_output_aliases={2: 0},
        compiler_params=_cparams("arbitrary"),
        name="moe_dispatch",
    )(pos3, x1, xs0)


def _new_expert(te_ref, i):
    return (i == 0) | (te_ref[i] != te_ref[jnp.maximum(i - 1, 0)])


def _expert_up_kernel(te_ref, nu_ref, x_ref, wg_ref, wu_ref, bg_ref, bu_ref, o_ref, wgb_ref, wub_ref):
    i = pl.program_id(1)

    @pl.when(_new_expert(te_ref, i))
    def _():
        wgb_ref[...] = wg_ref[...].astype(BF16)
        wub_ref[...] = wu_ref[...].astype(BF16)

    @pl.when(i < nu_ref[0])
    def _():
        xb = x_ref[...].astype(BF16)
        g = _dot(xb, wgb_ref[...]) + bg_ref[...]
        u = _dot(xb, wub_ref[...]) + bu_ref[...]
        g = jnp.minimum(g, SWIGLU_LIMIT)
        u = jnp.clip(u, -SWIGLU_LIMIT, SWIGLU_LIMIT)
        o_ref[...] = ((u + 1.0) * g * _sigmoid(SWIGLU_ALPHA * g)).astype(o_ref.dtype)


def _expert_up(te, nu, xs, w_gu, b_gu, splits=4):
    p, d = xs.shape
    ff = w_gu.shape[2] // 2
    fj = ff // splits
    tm = MOE_TM
    tile_of = lambda i, nu: jnp.minimum(i, nu[0] - 1)
    grid_spec = pltpu.PrefetchScalarGridSpec(
        num_scalar_prefetch=2,
        grid=(splits, p // tm),
        in_specs=[pl.BlockSpec((tm, d), lambda j, i, te, nu: (tile_of(i, nu), 0)),
                  pl.BlockSpec((None, d, fj), lambda j, i, te, nu: (te[i], 0, j)),
                  pl.BlockSpec((None, d, fj), lambda j, i, te, nu: (te[i], 0, splits + j)),
                  pl.BlockSpec((None, 1, fj), lambda j, i, te, nu: (te[i], 0, j)),
                  pl.BlockSpec((None, 1, fj), lambda j, i, te, nu: (te[i], 0, splits + j))],
        out_specs=pl.BlockSpec((tm, fj), lambda j, i, te, nu: (tile_of(i, nu), j)),
        scratch_shapes=[pltpu.VMEM((d, fj), BF16), pltpu.VMEM((d, fj), BF16)],
    )
    return pl.pallas_call(
        _expert_up_kernel,
        grid_spec=grid_spec,
        out_shape=jax.ShapeDtypeStruct((p, ff), BF16),
        compiler_params=_cparams("arbitrary", "arbitrary"),
        name="expert_up",
    )(te, nu, xs, w_gu, w_gu, b_gu, b_gu)


def _expert_down_kernel(te_ref, nu_ref, a_ref, wd_ref, bd_ref, o_ref, wdb_ref):
    i = pl.program_id(1)

    @pl.when(_new_expert(te_ref, i))
    def _():
        wdb_ref[...] = wd_ref[...].astype(BF16)

    @pl.when(i < nu_ref[0])
    def _():
        o_ref[...] = _dot(a_ref[...], wdb_ref[...]) + bd_ref[...]


def _expert_down(te, nu, act, w_down, b_down, splits=2):
    p, ff = act.shape
    d = w_down.shape[2]
    dj = d // splits
    tm = MOE_TM
    tile_of = lambda i, nu: jnp.minimum(i, nu[0] - 1)
    grid_spec = pltpu.PrefetchScalarGridSpec(
        num_scalar_prefetch=2,
        grid=(splits, p // tm),
        in_specs=[pl.BlockSpec((tm, ff), lambda j, i, te, nu: (tile_of(i, nu), 0)),
                  pl.BlockSpec((None, ff, dj), lambda j, i, te, nu: (te[i], 0, j)),
                  pl.BlockSpec((None, 1, dj), lambda j, i, te, nu: (te[i], 0, j))],
        out_specs=pl.BlockSpec((tm, dj), lambda j, i, te, nu: (tile_of(i, nu), j)),
        scratch_shapes=[pltpu.VMEM((ff, dj), BF16)],
    )
    return pl.pallas_call(
        _expert_down_kernel,
        grid_spec=grid_spec,
        out_shape=jax.ShapeDtypeStruct((p, d), F32),
        compiler_params=_cparams("arbitrary", "arbitrary"),
        name="expert_down",
    )(te, nu, act, w_down, b_down)


def _combine_kernel(pos_ref, x_ref, gt_ref, g_ref, b_ref, eo_ref, o_ref, ob_ref, buf_ref, sem, *, alpha):
    tc = x_ref.shape[0]

    def issue(r, carry):
        for k in range(TOP_K):
            _row_copy(eo_ref, pos_ref[0, 0, r * TOP_K + k], buf_ref.at[k], r, sem).start()
        return carry

    lax.fori_loop(0, tc, issue, 0)

    def drain(r, carry):
        for k in range(TOP_K):
            _row_copy(eo_ref, 0, buf_ref.at[k], 0, sem).wait()
        return carry

    lax.fori_loop(0, tc, drain, 0)

    gt = gt_ref[...]
    y = alpha * x_ref[...]
    for k in range(TOP_K):
        y = y + gt[:, k:k + 1] * buf_ref[k]
    mu = jnp.mean(y, axis=-1, keepdims=True)
    yc = y - mu
    var = jnp.mean(yc * yc, axis=-1, keepdims=True)
    out = yc * lax.rsqrt(var + 1e-5) * g_ref[...] + b_ref[...]
    o_ref[...] = out
    ob_ref[...] = out.astype(ob_ref.dtype)


def _combine_ln(pos, x1, gates, g, b, eo, alpha, tile=256):
    n, d = x1.shape
    pos3 = pos.reshape(n // tile, 1, tile * TOP_K)
    row = lambda w: pl.BlockSpec((1, w), lambda i: (0, 0))
    return pl.pallas_call(
        functools.partial(_combine_kernel, alpha=alpha),
        grid=(n // tile,),
        in_specs=[pl.BlockSpec((1, 1, tile * TOP_K), lambda i: (i, 0, 0), memory_space=pltpu.SMEM),
                  pl.BlockSpec((tile, d), lambda i: (i, 0)),
                  pl.BlockSpec((tile, LANES), lambda i: (i, 0)),
                  row(d), row(d),
                  pl.BlockSpec(memory_space=pl.ANY)],
        out_specs=[pl.BlockSpec((tile, d), lambda i: (i, 0)),
                   pl.BlockSpec((tile, d), lambda i: (i, 0))],
        out_shape=[jax.ShapeDtypeStruct((n, d), F32), jax.ShapeDtypeStruct((n, d), BF16)],
        scratch_shapes=[pltpu.VMEM((TOP_K, tile, d), F32), pltpu.SemaphoreType.DMA(())],
        compiler_params=_cparams("arbitrary"),
        name="moe_combine_ln",
    )(pos3, x1, gates, g, b, eo)


def _slot_positions(top_i, n_tiles):
    tm = MOE_TM
    e_flat = top_i.reshape(-1)
    onehot = (e_flat[:, None] == jnp.arange(N_EXPERTS, dtype=I32)[None, :]).astype(I32)
    csum = jnp.cumsum(onehot, axis=0)
    rank = jnp.sum(csum * onehot, axis=1) - 1
    counts = csum[-1]
    padded = ((counts + tm - 1) // tm) * tm
    ends = jnp.cumsum(padded)
    pos = (ends - padded)[e_flat] + rank
    n_used = (ends[-1] // tm).astype(I32)
    tile_e = jnp.sum((ends[None, :] <= (jnp.arange(n_tiles, dtype=I32) * tm)[:, None]).astype(I32), axis=1)
    last_e = jnp.max(jnp.where(counts > 0, jnp.arange(N_EXPERTS, dtype=I32), 0))
    tile_e = jnp.minimum(tile_e, last_e)
    return pos.astype(I32), tile_e, n_used.reshape(1)


def _in_proj_columns():
    gw = GROUP_WIDTH
    gm, ssm = 0, 2 * gw
    xbc_w = gw + 4 * SSM_STATE
    ml = ssm + gw + xbc_w + SSM_HEADS
    att = ml + 4 * gw + 2 * ML_HEADS
    r = lambda a, b: list(range(a, b))
    big = (r(gm, gm + 2 * gw) + r(ssm + gw, ssm + gw + xbc_w) + r(ssm, ssm + gw) + r(ml, ml + 4 * gw)
           + r(att + Q_RANK, att + Q_RANK + 2 * gw) + r(att, att + Q_RANK))
    kidx0 = att + Q_RANK + 2 * gw
    small = {0: r(kidx0, kidx0 + IDX_DIM + IDX_HEADS),
             LANES: r(ssm + gw + xbc_w, ssm + gw + xbc_w + SSM_HEADS),
             2 * LANES: r(ml + 4 * gw, ml + 4 * gw + 2 * ML_HEADS)}
    return np.asarray(big, np.int32), small


def _pad_row(v, width, fill=0.0):
    v = v.reshape(1, -1).astype(F32)
    return jnp.pad(v, ((0, 0), (0, width - v.shape[1])), constant_values=fill)


def _rope_rows():
    def rows(rot, starts):
        half = rot // 2
        inv = ROPE_THETA ** (-jnp.arange(half, dtype=F32) * 2.0 / rot)
        f = jnp.zeros((LANES,), F32)
        s = jnp.zeros((LANES,), F32)
        for st in starts:
            f = f.at[st:st + half].set(inv).at[st + half:st + rot].set(inv)
            s = s.at[st:st + half].set(-1.0).at[st + half:st + rot].set(1.0)
        return f.reshape(1, LANES), s.reshape(1, LANES)
    fm, sm = rows(ATT_DIM // 4, [0])
    fi, si = rows(IDX_DIM // 4, [0, IDX_DIM])
    fk, sk = rows(IDX_DIM // 4, [0])
    return fm, sm, fi, si, fk, sk


def kernel(x, positions, w_in, gm_ln_g, gm_ln_b, gm_ws, gm_bs, ssm_conv_w, ssm_conv_b, ssm_dt_bias, ssm_a_log,
           ssm_d, ssm_norm_g, ml_b_i, ml_b_f, ml_norm_g, att_q_norm_g, att_w_uq, idx_k_ln_g, idx_k_ln_b, w_out,
           ln1_g, ln1_b, ln2_g, ln2_b, router_w, router_b, expert_w_gu, expert_b_gu, expert_w_down,
           expert_b_down):
    bsz, seq, d = x.shape
    depth = w_in.shape[0]
    n = bsz * seq
    alpha = (2.0 * depth) ** 0.25
    n_slots = n * TOP_K + N_EXPERTS * MOE_TM
    n_tiles = n_slots // MOE_TM
    big_cols, small_cols = _in_proj_columns()
    pos = positions.reshape(n, 1).astype(I32)

    xf = x.reshape(n, d)
    xb = xf.astype(BF16)
    for l in range(depth):
        w_big = jnp.pad(jnp.take(w_in[l], big_cols, axis=1), ((0, 0), (0, H_COLS - big_cols.size))).astype(BF16)
        w_small = jnp.zeros((d, HS_COLS), F32)
        for c0, cols in small_cols.items():
            w_small = w_small.at[:, c0:c0 + len(cols)].set(jnp.take(w_in[l], np.asarray(cols, np.int32), axis=1))
        h = _matmul(xb, w_big, BF16, 1024, 1024)
        hs = _matmul(xb, w_small.astype(BF16), F32, 1024, HS_COLS)

        y_gm = _gmlp(h, gm_ln_g[l].reshape(1, -1), gm_ln_b[l].reshape(1, -1), gm_ws[l],
                     gm_bs[l].reshape(GM_HEADS, GM_BLOCK, 1))
        y_ssm = _ssd(h, hs, bsz, seq, ssm_conv_w[l], ssm_conv_b[l].reshape(1, -1),
                     _pad_row(ssm_dt_bias[l], LANES), _pad_row(-jnp.exp(ssm_a_log[l].astype(F32)), LANES),
                     jnp.repeat(ssm_d[l].astype(F32), SSM_HEADDIM).reshape(1, -1), ssm_norm_g[l].reshape(1, -1))
        y_ml = _mlstm(h, hs, bsz, seq, _pad_row(jnp.concatenate([ml_b_i[l], ml_b_f[l]]), LANES),
                      ml_norm_g[l].reshape(1, -1))
        y_att = _dsa(h, hs, pos, bsz, seq, att_q_norm_g[l], att_w_uq[l], idx_k_ln_g[l], idx_k_ln_b[l])

        x1, top_i, gates = _mix_ln_router(
            (y_gm, y_ssm, y_ml, y_att), xf, w_out[l].astype(BF16), ln1_g[l].reshape(1, -1), ln1_b[l].reshape(1, -1),
            jnp.pad(router_w[l], ((0, 0), (0, LANES - N_EXPERTS))),
            _pad_row(router_b[l], LANES, -jnp.inf), alpha)

        slot, tile_e, n_used = _slot_positions(top_i[:, :TOP_K], n_tiles)
        xs = _dispatch(slot, x1, n_slots)
        act = _expert_up(tile_e, n_used, xs, expert_w_gu[l], expert_b_gu[l][:, None, :])
        eo = _expert_down(tile_e, n_used, act, expert_w_down[l], expert_b_down[l][:, None, :])
        xf, xb = _combine_ln(slot, x1, gates, ln2_g[l].reshape(1, -1), ln2_b[l].reshape(1, -1), eo, alpha)
    return xf.reshape(bsz, seq, d)
```

```python
import functools
import math

import numpy as np
import jax
import jax.numpy as jnp
from jax import lax
from jax.experimental import pallas as pl
from jax.experimental.pallas import tpu as pltpu

F32 = jnp.float32
BF16 = jnp.bfloat16
I32 = jnp.int32

LANES = 128
CHUNK = 64
GROUP_WIDTH = 512
GM_BLOCK = 128
GM_HEADS = 4
SSM_HEADS = 8
SSM_HEADDIM = 64
SSM_STATE = 128
SSM_CONV = 4
ML_HEADS = 4
ML_DIM = 128
ATT_HEADS = 4
ATT_DIM = 128
Q_RANK = 384
IDX_HEADS = 8
IDX_DIM = 64
IDX_TOPK_MAX = 256
ROPE_THETA = 500000.0
N_EXPERTS = 32
TOP_K = 4
SWIGLU_LIMIT = 7.0
SWIGLU_ALPHA = 1.702

SEQ_CHUNK = 128
SEQ_TILE = 512
DSA_SB = 128
DSA_QB = 512
DSA_TK = 512
DSA_TILE_GROUP = 4
MOE_TM = 512
VMEM_LIMIT = 56 * 1024 * 1024

NEG_BIG = -1e30
INT_MIN = -(2 ** 31)
INT_MAX = 2 ** 31 - 1
NEG_INF_KEY = int(np.int32(np.uint32(0xFF800000) ^ np.uint32(0x7FFFFFFF)))

H_GM, H_XBC, H_Z, H_Q, H_K, H_V, H_O, H_AK, H_AV, H_CQ, H_COLS = (
    0, 1024, 2048, 2560, 3072, 3584, 4096, 4608, 5120, 5632, 6144)
HS_COLS = 384


def _cparams(*sem):
    return pltpu.CompilerParams(dimension_semantics=sem, vmem_limit_bytes=VMEM_LIMIT)


def _dot(a, b):
    return jnp.dot(a, b, preferred_element_type=F32)


def _dot_nt(a, b):
    return lax.dot_general(a, b, (((1,), (1,)), ((), ())), preferred_element_type=F32)


def _dot_exact_lhs(tri, x):
    hi = x.astype(BF16)
    r1 = x - hi.astype(F32)
    mid = r1.astype(BF16)
    lo = (r1 - mid.astype(F32)).astype(BF16)
    return _dot(tri, hi) + _dot(tri, mid) + _dot(tri, lo)


def _lower_tri(n):
    r = lax.broadcasted_iota(I32, (n, n), 0)
    c = lax.broadcasted_iota(I32, (n, n), 1)
    return r >= c


def _sigmoid(x):
    return 1.0 / (1.0 + jnp.exp(-x))


def _softplus(x):
    return jnp.maximum(x, 0.0) + jnp.log(1.0 + jnp.exp(-jnp.abs(x)))


def _log_sigmoid(x):
    return -_softplus(-x)


def _mm_kernel(a_ref, b_ref, o_ref):
    o_ref[...] = _dot(a_ref[...], b_ref[...]).astype(o_ref.dtype)


def _matmul(a, b, out_dtype, tm, tn):
    m, k = a.shape
    n = b.shape[1]
    return pl.pallas_call(
        _mm_kernel,
        grid=(n // tn, m // tm),
        in_specs=[pl.BlockSpec((tm, k), lambda j, i: (i, 0)),
                  pl.BlockSpec((k, tn), lambda j, i: (0, j))],
        out_specs=pl.BlockSpec((tm, tn), lambda j, i: (i, j)),
        out_shape=jax.ShapeDtypeStruct((m, n), out_dtype),
        compiler_params=_cparams("parallel", "parallel"),
        name="in_proj",
    )(a, b)


def _gmlp_kernel(h_ref, lng_ref, lnb_ref, ws_ref, bs_ref, o_ref):
    t = h_ref.shape[0]
    h = h_ref[...].astype(F32)
    h = 0.5 * h * (1.0 + lax.erf(h * (1.0 / math.sqrt(2.0))))
    r = lax.broadcasted_iota(I32, (GM_BLOCK, GM_BLOCK), 0)
    c = lax.broadcasted_iota(I32, (GM_BLOCK, GM_BLOCK), 1)
    allowed = (c // CHUNK) <= (r // CHUNK)
    for g in range(GM_HEADS):
        u = h[:, g * LANES:(g + 1) * LANES]
        v = h[:, GROUP_WIDTH + g * LANES:GROUP_WIDTH + (g + 1) * LANES]
        mu = jnp.mean(v, axis=-1, keepdims=True)
        vc = v - mu
        var = jnp.mean(vc * vc, axis=-1, keepdims=True)
        vn = vc * lax.rsqrt(var + 1e-5) * lng_ref[:, g * LANES:(g + 1) * LANES] \
            + lnb_ref[:, g * LANES:(g + 1) * LANES]
        w = jnp.where(allowed, ws_ref[g], 0.0).astype(BF16)
        for wdw in range(t // GM_BLOCK):
            rows = slice(wdw * GM_BLOCK, (wdw + 1) * GM_BLOCK)
            vmix = _dot(w, vn[rows].astype(BF16)) + bs_ref[g]
            o_ref[rows, g * LANES:(g + 1) * LANES] = (u[rows] * vmix).astype(o_ref.dtype)


def _gmlp(h, lng, lnb, ws, bs, tile=256):
    n = h.shape[0]
    return pl.pallas_call(
        _gmlp_kernel,
        grid=(n // tile,),
        in_specs=[pl.BlockSpec((tile, 2 * GROUP_WIDTH), lambda i: (i, H_GM // (2 * GROUP_WIDTH))),
                  pl.BlockSpec((1, GROUP_WIDTH), lambda i: (0, 0)),
                  pl.BlockSpec((1, GROUP_WIDTH), lambda i: (0, 0)),
                  pl.BlockSpec((GM_HEADS, GM_BLOCK, GM_BLOCK), lambda i: (0, 0, 0)),
                  pl.BlockSpec((GM_HEADS, GM_BLOCK, 1), lambda i: (0, 0, 0))],
        out_specs=pl.BlockSpec((tile, GROUP_WIDTH), lambda i: (i, 0)),
        out_shape=jax.ShapeDtypeStruct((n, GROUP_WIDTH), BF16),
        compiler_params=_cparams("parallel"),
        name="gmlp",
    )(h, lng, lnb, ws, bs)


def _ssd_kernel(xbc_ref, z_ref, dt_ref, cw_ref, cb_ref, dtb_ref, a_ref, dsk_ref, ng_ref, o_ref,
                buf_ref, act_ref, st_ref):
    t = xbc_ref.shape[0]
    lc = SEQ_CHUNK

    @pl.when(pl.program_id(1) == 0)
    def _():
        buf_ref[0:8, :] = jnp.zeros((8, buf_ref.shape[1]), F32)
        st_ref[...] = jnp.zeros(st_ref.shape, F32)

    buf_ref[8:8 + t, :] = xbc_ref[...].astype(F32)
    conv = cb_ref[...] + cw_ref[0:1, :] * buf_ref[5:5 + t, :]
    for k in range(1, SSM_CONV):
        conv = conv + cw_ref[k:k + 1, :] * buf_ref[5 + k:5 + k + t, :]
    act_ref[...] = conv * _sigmoid(conv)
    buf_ref[0:8, :] = buf_ref[t:t + 8, :]

    tri = _lower_tri(lc)
    tri_b = tri.astype(BF16)
    lane = lax.broadcasted_iota(I32, (lc, LANES), 1)
    first_half = lane < SSM_HEADDIM
    a_row = a_ref[...]

    def chunk(ci, carry):
        r0 = pl.multiple_of(ci * lc, lc)
        xa = act_ref[pl.ds(r0, lc), :]
        dt = _softplus(dt_ref[pl.ds(r0, lc), :] + dtb_ref[...])
        acs = _dot_exact_lhs(tri_b, dt * a_row)
        acs_t = acs.T
        ys = []
        for p in range(SSM_HEADS // 2):
            g = p // 2
            h0, h1 = 2 * p, 2 * p + 1
            bm = xa[:, GROUP_WIDTH + g * SSM_STATE:GROUP_WIDTH + (g + 1) * SSM_STATE]
            cm = xa[:, GROUP_WIDTH + 2 * SSM_STATE + g * SSM_STATE:GROUP_WIDTH + 2 * SSM_STATE + (g + 1) * SSM_STATE]
            cm_b = cm.astype(BF16)
            cb = _dot_nt(cm_b, bm.astype(BF16))
            x2 = xa[:, p * LANES:(p + 1) * LANES]
            dt2 = jnp.where(first_half, dt[:, h0:h0 + 1], dt[:, h1:h1 + 1])
            xdt = x2 * dt2
            xdt_b = xdt.astype(BF16)
            yd = []
            for hh in (h0, h1):
                decay = jnp.where(tri, jnp.exp(acs[:, hh:hh + 1] - acs_t[hh:hh + 1, :]), 0.0)
                yd.append(_dot((cb * decay).astype(BF16), xdt_b))
            y_diag = jnp.where(first_half, yd[0], yd[1])
            acs2 = jnp.where(first_half, acs[:, h0:h0 + 1], acs[:, h1:h1 + 1])
            aend2 = jnp.where(first_half[0:1], acs[lc - 1:lc, h0:h0 + 1], acs[lc - 1:lc, h1:h1 + 1])
            st = st_ref[p]
            y_off = _dot(cm_b, st.astype(BF16)) * jnp.exp(acs2)
            ys.append(y_diag + y_off + x2 * dsk_ref[:, p * LANES:(p + 1) * LANES])
            upd = _dot(bm.T.astype(BF16), (xdt * jnp.exp(aend2 - acs2)).astype(BF16))
            st_ref[p] = jnp.exp(aend2) * st + upd
        y = jnp.concatenate(ys, axis=-1)
        zz = z_ref[pl.ds(r0, lc), :].astype(F32)
        y = y * (zz * _sigmoid(zz))
        y = y * lax.rsqrt(jnp.mean(y * y, axis=-1, keepdims=True) + 1e-6) * ng_ref[...]
        o_ref[pl.ds(r0, lc), :] = y.astype(o_ref.dtype)
        return carry

    lax.fori_loop(0, t // lc, chunk, 0)


def _ssd(h, hs, bsz, seq, cw, cb, dtb, a_row, dsk, ng, tile=SEQ_TILE):
    n = h.shape[0]
    nt = seq // tile
    xw = 2 * GROUP_WIDTH
    return pl.pallas_call(
        _ssd_kernel,
        grid=(bsz, nt),
        in_specs=[pl.BlockSpec((tile, xw), lambda b, j: (b * nt + j, H_XBC // xw)),
                  pl.BlockSpec((tile, GROUP_WIDTH), lambda b, j: (b * nt + j, H_Z // GROUP_WIDTH)),
                  pl.BlockSpec((tile, LANES), lambda b, j: (b * nt + j, 1)),
                  pl.BlockSpec((SSM_CONV, xw), lambda b, j: (0, 0)),
                  pl.BlockSpec((1, xw), lambda b, j: (0, 0)),
                  pl.BlockSpec((1, LANES), lambda b, j: (0, 0)),
                  pl.BlockSpec((1, LANES), lambda b, j: (0, 0)),
                  pl.BlockSpec((1, GROUP_WIDTH), lambda b, j: (0, 0)),
                  pl.BlockSpec((1, GROUP_WIDTH), lambda b, j: (0, 0))],
        out_specs=pl.BlockSpec((tile, GROUP_WIDTH), lambda b, j: (b * nt + j, 0)),
        out_shape=jax.ShapeDtypeStruct((n, GROUP_WIDTH), BF16),
        scratch_shapes=[pltpu.VMEM((tile + 8, xw), F32),
                        pltpu.VMEM((tile, xw), F32),
                        pltpu.VMEM((SSM_HEADS // 2, SSM_STATE, LANES), F32)],
        compiler_params=_cparams("arbitrary", "arbitrary"),
        name="ssd",
    )(h, h, hs, cw, cb, dtb, a_row, dsk, ng)


def _mlstm_kernel(q_ref, k_ref, v_ref, og_ref, gt_ref, bias_ref, ng_ref, o_ref, c_ref, n_ref, m_ref):
    t = q_ref.shape[0]
    lc = SEQ_CHUNK
    scale = ML_DIM ** -0.5

    @pl.when(pl.program_id(1) == 0)
    def _():
        c_ref[...] = jnp.zeros(c_ref.shape, F32)
        n_ref[...] = jnp.zeros(n_ref.shape, F32)
        m_ref[...] = jnp.zeros(m_ref.shape, F32)

    tri = _lower_tri(lc)
    tri_b = tri.astype(BF16)

    def chunk(ci, carry):
        r0 = pl.multiple_of(ci * lc, lc)
        gts = gt_ref[pl.ds(r0, lc), :] + bias_ref[...]
        bcum = _dot_exact_lhs(tri_b, _log_sigmoid(gts))
        gts_t = gts.T
        bcum_t = bcum.T
        for hd in range(ML_HEADS):
            cols = slice(hd * ML_DIM, (hd + 1) * ML_DIM)
            qh = q_ref[pl.ds(r0, lc), cols]
            kh = (k_ref[pl.ds(r0, lc), cols].astype(F32) * scale).astype(BF16)
            vh = v_ref[pl.ds(r0, lc), cols]
            i_col = gts[:, hd:hd + 1]
            i_row = gts_t[hd:hd + 1, :]
            b_col = bcum[:, ML_HEADS + hd:ML_HEADS + hd + 1]
            b_row = bcum_t[ML_HEADS + hd:ML_HEADS + hd + 1, :]
            b_end = b_col[lc - 1:lc, :]
            m_prev = m_ref[hd][0:1, 0:1]
            c_prev = c_ref[hd]
            n_prev = n_ref[hd][0:1, :]
            log_d = jnp.where(tri, b_col - b_row + i_row, -jnp.inf)
            inter = b_col + m_prev
            m_j = jnp.maximum(jnp.max(log_d, axis=-1, keepdims=True), inter)
            s_mat = jnp.exp(log_d - m_j) * _dot_nt(qh, kh)
            gsc = jnp.exp(inter - m_j)
            num = _dot(s_mat.astype(BF16), vh) + gsc * _dot_nt(qh, c_prev.astype(BF16))
            den = jnp.sum(s_mat, axis=-1, keepdims=True) \
                + gsc * jnp.sum(qh.astype(F32) * n_prev, axis=-1, keepdims=True)
            hh = num / jnp.maximum(jnp.abs(den), jnp.exp(-m_j))
            mu = jnp.mean(hh, axis=-1, keepdims=True)
            hc = hh - mu
            var = jnp.mean(hc * hc, axis=-1, keepdims=True)
            hn = hc * lax.rsqrt(var + 1e-5) * ng_ref[:, cols]
            og = og_ref[pl.ds(r0, lc), cols].astype(F32)
            o_ref[pl.ds(r0, lc), cols] = (hn * _sigmoid(og)).astype(o_ref.dtype)
            a_col = b_end - b_col + i_col
            a_row = b_end - b_row + i_row
            m_new = jnp.maximum(b_end + m_prev, jnp.max(a_row, axis=-1, keepdims=True))
            decay = jnp.exp(b_end + m_prev - m_new)
            w_col = jnp.exp(a_col - m_new)
            kf = kh.astype(F32)
            vw_t = (vh.astype(F32) * w_col).T.astype(BF16)
            c_ref[hd] = decay * c_prev + _dot(vw_t, kh)
            n_new = decay * n_prev + jnp.sum(w_col * kf, axis=0, keepdims=True)
            n_ref[hd] = jnp.broadcast_to(n_new, (8, ML_DIM))
            m_ref[hd] = jnp.broadcast_to(m_new, (8, LANES))
        return carry

    lax.fori_loop(0, t // lc, chunk, 0)


def _mlstm(h, hs, bsz, seq, bias, ng, tile=SEQ_TILE):
    n = h.shape[0]
    nt = seq // tile
    gw = GROUP_WIDTH

    def col(c0):
        return pl.BlockSpec((tile, gw), lambda b, j: (b * nt + j, c0 // gw))

    return pl.pallas_call(
        _mlstm_kernel,
        grid=(bsz, nt),
        in_specs=[col(H_Q), col(H_K), col(H_V), col(H_O),
                  pl.BlockSpec((tile, LANES), lambda b, j: (b * nt + j, 2)),
                  pl.BlockSpec((1, LANES), lambda b, j: (0, 0)),
                  pl.BlockSpec((1, gw), lambda b, j: (0, 0))],
        out_specs=pl.BlockSpec((tile, gw), lambda b, j: (b * nt + j, 0)),
        out_shape=jax.ShapeDtypeStruct((n, gw), BF16),
        scratch_shapes=[pltpu.VMEM((ML_HEADS, ML_DIM, ML_DIM), F32),
                        pltpu.VMEM((ML_HEADS, 8, ML_DIM), F32),
                        pltpu.VMEM((ML_HEADS, 8, LANES), F32)],
        compiler_params=_cparams("arbitrary", "arbitrary"),
        name="mlstm",
    )(h, h, h, h, hs, bias, ng)


def _att_prep_kernel(cq_ref, ak_ref, hs_ref, pos_ref, qg_ref, wuq_ref, kg_ref, kb_ref,
                     fm_ref, sm_ref, fi_ref, si_ref, fk_ref, sk_ref,
                     q_ref, k_ref, qi_ref, ki_ref, w_ref):
    t = cq_ref.shape[0]
    lane = lax.broadcasted_iota(I32, (t, LANES), 1)
    pos = pos_ref[...].astype(F32)

    cq = cq_ref[:, 0:Q_RANK].astype(F32)
    cq = cq * lax.rsqrt(jnp.mean(cq * cq, axis=-1, keepdims=True) + 1e-6) * qg_ref[...]
    q_all = _dot(cq.astype(BF16), wuq_ref[...])

    def rope(x, cos, sin_signed, half, period):
        first = (lane % period) < half
        partner = jnp.where(first, pltpu.roll(x, LANES - half, 1), pltpu.roll(x, half, 1))
        return x * cos + partner * sin_signed

    ang = pos * fm_ref[...]
    cos_m, sin_m = jnp.cos(ang), jnp.sin(ang) * sm_ref[...]
    half_m = ATT_DIM // 8
    for hd in range(ATT_HEADS):
        cols = slice(hd * ATT_DIM, (hd + 1) * ATT_DIM)
        q_ref[:, cols] = (rope(q_all[:, cols], cos_m, sin_m, half_m, LANES) * (ATT_DIM ** -0.5)).astype(q_ref.dtype)
        k_ref[:, cols] = rope(ak_ref[:, cols].astype(F32), cos_m, sin_m, half_m, LANES).astype(k_ref.dtype)

    ang = pos * fi_ref[...]
    cos_i, sin_i = jnp.cos(ang), jnp.sin(ang) * si_ref[...]
    half_i = IDX_DIM // 8
    low = lane < IDX_DIM
    for pr in range(IDX_HEADS // 2):
        x = rope(q_all[:, GROUP_WIDTH + pr * LANES:GROUP_WIDTH + (pr + 1) * LANES], cos_i, sin_i, half_i, IDX_DIM)
        even = jnp.where(low, x, 0.0).astype(qi_ref.dtype)
        odd = jnp.where(low, pltpu.roll(x, IDX_DIM, 1), 0.0).astype(qi_ref.dtype)
        for blk in range(t // DSA_SB):
            rows = slice(blk * DSA_SB, (blk + 1) * DSA_SB)
            qi_ref[blk, 2 * pr] = even[rows]
            qi_ref[blk, 2 * pr + 1] = odd[rows]

    hs = hs_ref[...]
    kx = jnp.where(low, hs, 0.0)
    mu = jnp.sum(kx, axis=-1, keepdims=True) * (1.0 / IDX_DIM)
    kc = jnp.where(low, hs - mu, 0.0)
    var = jnp.sum(kc * kc, axis=-1, keepdims=True) * (1.0 / IDX_DIM)
    kn = kc * lax.rsqrt(var + 1e-5) * kg_ref[...] + kb_ref[...]
    ang = pos * fk_ref[...]
    ki_ref[...] = rope(kn, jnp.cos(ang), jnp.sin(ang) * sk_ref[...], half_i, LANES).astype(ki_ref.dtype)
    wi = jnp.where(lane < IDX_HEADS, pltpu.roll(hs, LANES - IDX_DIM, 1), 0.0) * (IDX_HEADS ** -0.5 * IDX_DIM ** -0.5)
    for blk in range(t // DSA_SB):
        w_ref[blk] = wi[blk * DSA_SB:(blk + 1) * DSA_SB, :].T[0:IDX_HEADS, :]


def _att_prep(h, hs, pos, qg, wuq, kg, kb, rope_rows, tile=256):
    n = h.shape[0]
    gw = GROUP_WIDTH
    row = lambda w: pl.BlockSpec((1, w), lambda i: (0, 0))
    return pl.pallas_call(
        _att_prep_kernel,
        grid=(n // tile,),
        in_specs=[pl.BlockSpec((tile, gw), lambda i: (i, H_CQ // gw)),
                  pl.BlockSpec((tile, gw), lambda i: (i, H_AK // gw)),
                  pl.BlockSpec((tile, LANES), lambda i: (i, 0)),
                  pl.BlockSpec((tile, 1), lambda i: (i, 0)),
                  row(Q_RANK),
                  pl.BlockSpec(wuq.shape, lambda i: (0, 0)),
                  row(LANES), row(LANES)] + [row(LANES)] * 6,
        out_specs=[pl.BlockSpec((tile, gw), lambda i: (i, 0)),
                   pl.BlockSpec((tile, gw), lambda i: (i, 0)),
                   pl.BlockSpec((tile // DSA_SB, IDX_HEADS, DSA_SB, LANES), lambda i: (i, 0, 0, 0)),
                   pl.BlockSpec((tile, LANES), lambda i: (i, 0)),
                   pl.BlockSpec((tile // DSA_SB, IDX_HEADS, DSA_SB), lambda i: (i, 0, 0))],
        out_shape=[jax.ShapeDtypeStruct((n, gw), BF16),
                   jax.ShapeDtypeStruct((n, gw), BF16),
                   jax.ShapeDtypeStruct((n // DSA_SB, IDX_HEADS, DSA_SB, LANES), BF16),
                   jax.ShapeDtypeStruct((n, LANES), BF16),
                   jax.ShapeDtypeStruct((n // DSA_SB, IDX_HEADS, DSA_SB), F32)],
        compiler_params=_cparams("parallel"),
        name="att_prep",
    )(h, h, hs, pos, qg, wuq, kg, kb, *rope_rows)


def _dsa_select_kernel(qi_ref, w_ref, ki_ref, bias_ref, key_ref, hi_ref, lo_ref, tri_ref, *, top_k):
    sb = w_ref.shape[2]
    tk = key_ref.shape[1]
    nt_all = bias_ref.shape[1]
    i = pl.program_id(1)
    n_tiles = (i * sb + sb + tk - 1) // tk
    row = lax.broadcasted_iota(I32, (1, sb), 1)
    lim = ((i * sb + row) // CHUNK + 1) * CHUNK
    sub_t = lax.broadcasted_iota(I32, (tk, sb), 0)
    w = w_ref[0]
    half_min = -(2 ** 15)

    def fill(kt, carry):
        k0 = pl.multiple_of(kt * tk, tk)
        kk = ki_ref[pl.ds(k0, tk), :]
        acc = None
        for pr in range(IDX_HEADS // 2):
            rel = _dot_nt(kk, qi_ref[2 * pr * sb:(2 * pr + 2) * sb, :])
            for u in range(2):
                term = jnp.maximum(rel[:, u * sb:(u + 1) * sb], 0.0) * w[2 * pr + u:2 * pr + u + 1, :]
                acc = term if acc is None else acc + term
        score = jnp.where(sub_t + k0 < lim, acc + 0.0, -jnp.inf)
        bits = pltpu.bitcast(score, I32)
        keys = bits ^ ((bits >> 31) & INT_MAX)
        key_ref[kt] = keys
        hi_ref[kt] = (keys >> 16).astype(jnp.int16)
        return carry

    lax.fori_loop(0, n_tiles, fill, 0)

    group = DSA_TILE_GROUP
    n_groups = (n_tiles + group - 1) // group

    def pad_tile(kt, carry):
        key_ref[kt] = jnp.full((tk, sb), NEG_INF_KEY, I32)
        hi_ref[kt] = jnp.full((tk, sb), NEG_INF_KEY >> 16, jnp.int16)
        return carry

    lax.fori_loop(n_tiles, n_groups * group, pad_tile, 0)

    def tree_sum(parts):
        while len(parts) > 1:
            parts = [parts[j] + parts[j + 1] for j in range(0, len(parts), 2)]
        return parts[0]

    def count(ref, pred, dtype):
        gran = 32 // jnp.dtype(dtype).itemsize

        def body(g, acc):
            for u in range(group):
                kt = g * group + u
                m = pred(ref[kt], kt).astype(dtype)
                acc = acc + tree_sum([m[c * gran:(c + 1) * gran] for c in range(tk // gran)])
            return acc
        acc = lax.fori_loop(0, n_groups, body, jnp.zeros((gran, sb), dtype))
        return jnp.sum(acc.astype(I32), axis=0, keepdims=True)

    def count16_ge(ref, cand):
        c16 = cand.astype(jnp.int16)
        return count(ref, lambda v, kt: v >= c16, jnp.int16)

    def greedy16(ref, need):
        def step(s, t):
            cand = t + lax.shift_left(jnp.int32(1), 15 - s)
            return jnp.where(count16_ge(ref, cand) >= need, cand, t)
        return lax.fori_loop(0, 16, step, jnp.full((1, sb), half_min, I32))

    t_hi = greedy16(hi_ref, top_k)
    t_hi16 = t_hi.astype(jnp.int16)
    above = count(hi_ref, lambda v, kt: v > t_hi16, jnp.int16)
    need_lo = top_k - above

    def fill_lo(kt, carry):
        lo = ((key_ref[kt] & 0xFFFF) + half_min).astype(jnp.int16)
        lo_ref[kt] = jnp.where(hi_ref[kt] == t_hi16, lo, jnp.int16(half_min))
        return carry

    lax.fori_loop(0, n_groups * group, fill_lo, 0)
    t_lo = greedy16(lo_ref, need_lo)
    thr = (t_hi << 16) | ((t_lo - half_min) & 0xFFFF)
    t_lo16 = t_lo.astype(jnp.int16)
    need_tied = (need_lo - count(lo_ref, lambda v, kt: v > t_lo16, jnp.int16)).astype(F32)

    r = lax.broadcasted_iota(I32, (tk, tk), 0)
    c = lax.broadcasted_iota(I32, (tk, tk), 1)
    tri_ref[...] = (r >= c).astype(tri_ref.dtype)

    def emit(kt, seen):
        keys = key_ref[kt]
        tied = keys == thr
        seen = seen + _dot(tri_ref[...], tied.astype(tri_ref.dtype))
        sel = ((keys > thr) | (tied & (seen <= need_tied))) & (sub_t + kt * tk < lim)
        bias_t = jnp.where(sel, 0.0, -jnp.inf)
        for cb in range(tk // LANES):
            bias_ref[0, kt, :, cb * LANES:(cb + 1) * LANES] = \
                bias_t[cb * LANES:(cb + 1) * LANES, :].T.astype(bias_ref.dtype)
        return seen[tk - 1:tk, :]

    lax.fori_loop(0, n_tiles, emit, jnp.zeros((1, sb), F32))

    def blank(kt, carry):
        bias_ref[0, kt] = jnp.full(bias_ref.shape[2:], -jnp.inf, bias_ref.dtype)
        return carry

    lax.fori_loop(n_tiles, nt_all, blank, 0)


def _dsa_select(qi, w, ki, bsz, seq, top_k):
    nb, _, sb = w.shape
    tk = DSA_TK
    nq, nt = seq // sb, seq // tk
    nt_pad = -(-nt // DSA_TILE_GROUP) * DSA_TILE_GROUP
    kern = functools.partial(_dsa_select_kernel, top_k=top_k)
    return pl.pallas_call(
        kern,
        grid=(bsz, nq),
        in_specs=[pl.BlockSpec((IDX_HEADS * sb, LANES), lambda b, i: (b * nq + i, 0)),
                  pl.BlockSpec((1, IDX_HEADS, sb), lambda b, i: (b * nq + i, 0, 0)),
                  pl.BlockSpec((seq, LANES), lambda b, i: (b, 0))],
        out_specs=pl.BlockSpec((1, nt, sb, tk), lambda b, i: (b * nq + i, 0, 0, 0)),
        out_shape=jax.ShapeDtypeStruct((nb, nt, sb, tk), BF16),
        scratch_shapes=[pltpu.VMEM((nt_pad, tk, sb), I32),
                        pltpu.VMEM((nt_pad, tk, sb), jnp.int16),
                        pltpu.VMEM((nt_pad, tk, sb), jnp.int16),
                        pltpu.VMEM((tk, tk), BF16)],
        compiler_params=_cparams("parallel", "parallel"),
        name="dsa_select",
    )(qi.reshape(nb * IDX_HEADS * sb, LANES), w, ki)


def _dsa_attn_kernel(qt_ref, kt_ref, first_ref, last_ref, q_ref, bias_ref, k_ref, v_ref, o_ref,
                     m_ref, l_ref, acc_ref):
    s = pl.program_id(1)
    qb, tk = q_ref.shape[0], k_ref.shape[0]

    @pl.when(first_ref[s] == 1)
    def _():
        m_ref[...] = jnp.full(m_ref.shape, NEG_BIG, F32)
        l_ref[...] = jnp.zeros(l_ref.shape, F32)
        acc_ref[...] = jnp.zeros(acc_ref.shape, F32)

    bias = bias_ref[...].reshape(qb, tk).astype(F32)

    for hd in range(ATT_HEADS):
        cols = slice(hd * ATT_DIM, (hd + 1) * ATT_DIM)
        logits = _dot_nt(q_ref[:, cols], k_ref[:, cols]) + bias
        m_prev = m_ref[:, cols]
        m_new = jnp.maximum(m_prev, jnp.max(logits, axis=-1, keepdims=True))
        p = jnp.exp(logits - m_new[:, 0:1])
        alpha = jnp.exp(m_prev - m_new)
        l_ref[:, cols] = alpha * l_ref[:, cols] + jnp.sum(p, axis=-1, keepdims=True)
        acc_ref[:, cols] = alpha * acc_ref[:, cols] + _dot(p.astype(BF16), v_ref[:, cols])
        m_ref[:, cols] = m_new

    @pl.when(last_ref[s] == 1)
    def _():
        o_ref[...] = (acc_ref[...] / l_ref[...]).astype(o_ref.dtype)


def _dsa_attention(q, bias, k, h, bsz, seq):
    n = q.shape[0]
    qb, tk = DSA_QB, DSA_TK
    nq, nkt = seq // qb, seq // tk
    sub = qb // DSA_SB
    pairs = [(i, t) for i in range(nq) for t in range((i * qb + qb - 1) // tk + 1)]
    qt = jnp.asarray([p[0] for p in pairs], I32)
    ktab = jnp.asarray([p[1] for p in pairs], I32)
    first = jnp.asarray([1 if p[1] == 0 else 0 for p in pairs], I32)
    last = jnp.asarray([1 if p[1] == (p[0] * qb + qb - 1) // tk else 0 for p in pairs], I32)
    gw = GROUP_WIDTH

    def qmap(w_):
        return pl.BlockSpec((qb, w_), lambda b, s, qt, kt, f, l: (b * nq + qt[s], 0))

    def kmap(w_, cb):
        return pl.BlockSpec((tk, w_), lambda b, s, qt, kt, f, l: (b * nkt + kt[s], cb))

    grid_spec = pltpu.PrefetchScalarGridSpec(
        num_scalar_prefetch=4,
        grid=(bsz, len(pairs)),
        in_specs=[qmap(gw),
                  pl.BlockSpec((sub, 1, DSA_SB, tk), lambda b, s, qt, kt, f, l: (b * nq + qt[s], kt[s], 0, 0)),
                  kmap(gw, 0), kmap(gw, H_AV // gw)],
        out_specs=qmap(gw),
        scratch_shapes=[pltpu.VMEM((qb, gw), F32), pltpu.VMEM((qb, gw), F32), pltpu.VMEM((qb, gw), F32)],
    )
    return pl.pallas_call(
        _dsa_attn_kernel,
        grid_spec=grid_spec,
        out_shape=jax.ShapeDtypeStruct((n, gw), BF16),
        compiler_params=_cparams("arbitrary", "arbitrary"),
        name="dsa_attention",
    )(qt, ktab, first, last, q, bias, k, h)


def _dsa(h, hs, pos, bsz, seq, q_norm_g, w_uq, idxk_g, idxk_b):
    q, k, qi, ki, wi = _att_prep(h, hs, pos, q_norm_g.reshape(1, -1), w_uq.astype(BF16),
                                 _pad_row(idxk_g, LANES), _pad_row(idxk_b, LANES), _rope_rows())
    bias = _dsa_select(qi, wi, ki, bsz, seq, min(IDX_TOPK_MAX, seq // 4))
    return _dsa_attention(q, bias, k, h, bsz, seq)


def _mix_kernel(ya_ref, yb_ref, yc_ref, yd_ref, x_ref, wo_ref, g_ref, b_ref, rw_ref, rb_ref,
                x1_ref, ti_ref, gt_ref, *, alpha):
    gw = GROUP_WIDTH
    mix = _dot(ya_ref[...], wo_ref[0:gw, :])
    mix = mix + _dot(yb_ref[...], wo_ref[gw:2 * gw, :])
    mix = mix + _dot(yc_ref[...], wo_ref[2 * gw:3 * gw, :])
    mix = mix + _dot(yd_ref[...], wo_ref[3 * gw:4 * gw, :])
    y = alpha * x_ref[...] + mix
    mu = jnp.mean(y, axis=-1, keepdims=True)
    yc = y - mu
    var = jnp.mean(yc * yc, axis=-1, keepdims=True)
    x1 = yc * lax.rsqrt(var + 1e-5) * g_ref[...] + b_ref[...]
    x1_ref[...] = x1

    logits = jnp.dot(x1, rw_ref[...], precision=lax.Precision.HIGHEST, preferred_element_type=F32) + rb_ref[...]
    lane = lax.broadcasted_iota(I32, logits.shape, 1)
    vals, idxs = [], []
    for _ in range(TOP_K):
        mx = jnp.max(logits, axis=-1, keepdims=True)
        ix = jnp.min(jnp.where(logits == mx, lane, LANES), axis=-1, keepdims=True)
        vals.append(mx)
        idxs.append(ix)
        logits = jnp.where(lane == ix, -jnp.inf, logits)
    es = [jnp.exp(v - vals[0]) for v in vals]
    tot = es[0] + es[1] + es[2] + es[3]
    ti = jnp.zeros(logits.shape, I32)
    gt = jnp.zeros(logits.shape, F32)
    for k in range(TOP_K):
        ti = jnp.where(lane == k, idxs[k], ti)
        gt = jnp.where(lane == k, es[k] / tot, gt)
    ti_ref[...] = ti
    gt_ref[...] = gt


def _mix_ln_router(ys, x, wo, g, b, rw, rb, alpha, tile=256):
    n, d = x.shape
    gw = GROUP_WIDTH
    yspec = pl.BlockSpec((tile, gw), lambda i: (i, 0))
    row = lambda w: pl.BlockSpec((1, w), lambda i: (0, 0))
    return pl.pallas_call(
        functools.partial(_mix_kernel, alpha=alpha),
        grid=(n // tile,),
        in_specs=[yspec, yspec, yspec, yspec,
                  pl.BlockSpec((tile, d), lambda i: (i, 0)),
                  pl.BlockSpec(wo.shape, lambda i: (0, 0)),
                  row(d), row(d),
                  pl.BlockSpec(rw.shape, lambda i: (0, 0)),
                  row(LANES)],
        out_specs=[pl.BlockSpec((tile, d), lambda i: (i, 0)),
                   pl.BlockSpec((tile, LANES), lambda i: (i, 0)),
                   pl.BlockSpec((tile, LANES), lambda i: (i, 0))],
        out_shape=[jax.ShapeDtypeStruct((n, d), F32),
                   jax.ShapeDtypeStruct((n, LANES), I32),
                   jax.ShapeDtypeStruct((n, LANES), F32)],
        compiler_params=_cparams("parallel"),
        name="mix_ln_router",
    )(*ys, x, wo, g, b, rw, rb)


def _row_copy(src, s, dst, d, sem):
    return pltpu.make_async_copy(src.at[pl.ds(s, 1), :], dst.at[pl.ds(d, 1), :], sem)


def _dispatch_kernel(pos_ref, x_ref, xs_in_ref, xs_ref, sem):
    del xs_in_ref
    tc = x_ref.shape[0]

    def issue(r, carry):
        for k in range(TOP_K):
            _row_copy(x_ref, r, xs_ref, pos_ref[0, 0, r * TOP_K + k], sem).start()
        return carry

    lax.fori_loop(0, tc, issue, 0)

    def drain(r, carry):
        for k in range(TOP_K):
            _row_copy(x_ref, 0, xs_ref, 0, sem).wait()
        return carry

    lax.fori_loop(0, tc, drain, 0)


def _dispatch(pos, x1, n_slots, tile=256):
    n, d = x1.shape
    pos3 = pos.reshape(n // tile, 1, tile * TOP_K)
    xs0 = jnp.zeros((n_slots, d), F32)
    return pl.pallas_call(
        _dispatch_kernel,
        grid=(n // tile,),
        in_specs=[pl.BlockSpec((1, 1, tile * TOP_K), lambda i: (i, 0, 0), memory_space=pltpu.SMEM),
                  pl.BlockSpec((tile, d), lambda i: (i, 0)),
                  pl.BlockSpec(memory_space=pl.ANY)],
        out_specs=pl.BlockSpec(memory_space=pl.ANY),
        out_shape=jax.ShapeDtypeStruct((n_slots, d), F32),
        scratch_shapes=[pltpu.SemaphoreType.DMA(())],
        input_output_aliases={2: 0},
        compiler_params=_cparams("arbitrary"),
        name="moe_dispatch",
    )(pos3, x1, xs0)


def _new_expert(te_ref, i):
    return (i == 0) | (te_ref[i] != te_ref[jnp.maximum(i - 1, 0)])


def _expert_up_kernel(te_ref, nu_ref, x_ref, wg_ref, wu_ref, bg_ref, bu_ref, o_ref, wgb_ref, wub_ref):
    i = pl.program_id(1)

    @pl.when(_new_expert(te_ref, i))
    def _():
        wgb_ref[...] = wg_ref[...].astype(BF16)
        wub_ref[...] = wu_ref[...].astype(BF16)

    @pl.when(i < nu_ref[0])
    def _():
        xb = x_ref[...].astype(BF16)
        g = _dot(xb, wgb_ref[...]) + bg_ref[...]
        u = _dot(xb, wub_ref[...]) + bu_ref[...]
        g = jnp.minimum(g, SWIGLU_LIMIT)
        u = jnp.clip(u, -SWIGLU_LIMIT, SWIGLU_LIMIT)
        o_ref[...] = ((u + 1.0) * g * _sigmoid(SWIGLU_ALPHA * g)).astype(o_ref.dtype)


def _expert_up(te, nu, xs, w_gu, b_gu, layer, splits=4):
    p, d = xs.shape
    ff = w_gu.shape[3] // 2
    fj = ff // splits
    tm = MOE_TM
    tile_of = lambda i, nu: jnp.minimum(i, nu[0] - 1)
    grid_spec = pltpu.PrefetchScalarGridSpec(
        num_scalar_prefetch=2,
        grid=(splits, p // tm),
        in_specs=[pl.BlockSpec((tm, d), lambda j, i, te, nu: (tile_of(i, nu), 0)),
                  pl.BlockSpec((None, None, d, fj), lambda j, i, te, nu: (layer, te[i], 0, j)),
                  pl.BlockSpec((None, None, d, fj), lambda j, i, te, nu: (layer, te[i], 0, splits + j)),
                  pl.BlockSpec((None, None, 1, fj), lambda j, i, te, nu: (layer, te[i], 0, j)),
                  pl.BlockSpec((None, None, 1, fj), lambda j, i, te, nu: (layer, te[i], 0, splits + j))],
        out_specs=pl.BlockSpec((tm, fj), lambda j, i, te, nu: (tile_of(i, nu), j)),
        scratch_shapes=[pltpu.VMEM((d, fj), BF16), pltpu.VMEM((d, fj), BF16)],
    )
    return pl.pallas_call(
        _expert_up_kernel,
        grid_spec=grid_spec,
        out_shape=jax.ShapeDtypeStruct((p, ff), BF16),
        compiler_params=_cparams("arbitrary", "arbitrary"),
        name="expert_up",
    )(te, nu, xs, w_gu, w_gu, b_gu, b_gu)


def _expert_down_kernel(te_ref, nu_ref, a_ref, wd_ref, bd_ref, o_ref, wdb_ref):
    i = pl.program_id(1)

    @pl.when(_new_expert(te_ref, i))
    def _():
        wdb_ref[...] = wd_ref[...].astype(BF16)

    @pl.when(i < nu_ref[0])
    def _():
        o_ref[...] = _dot(a_ref[...], wdb_ref[...]) + bd_ref[...]


def _expert_down(te, nu, act, w_down, b_down, layer, splits=2):
    p, ff = act.shape
    d = w_down.shape[3]
    dj = d // splits
    tm = MOE_TM
    tile_of = lambda i, nu: jnp.minimum(i, nu[0] - 1)
    grid_spec = pltpu.PrefetchScalarGridSpec(
        num_scalar_prefetch=2,
        grid=(splits, p // tm),
        in_specs=[pl.BlockSpec((tm, ff), lambda j, i, te, nu: (tile_of(i, nu), 0)),
                  pl.BlockSpec((None, None, ff, dj), lambda j, i, te, nu: (layer, te[i], 0, j)),
                  pl.BlockSpec((None, None, 1, dj), lambda j, i, te, nu: (layer, te[i], 0, j))],
        out_specs=pl.BlockSpec((tm, dj), lambda j, i, te, nu: (tile_of(i, nu), j)),
        scratch_shapes=[pltpu.VMEM((ff, dj), BF16)],
    )
    return pl.pallas_call(
        _expert_down_kernel,
        grid_spec=grid_spec,
        out_shape=jax.ShapeDtypeStruct((p, d), F32),
        compiler_params=_cparams("arbitrary", "arbitrary"),
        name="expert_down",
    )(te, nu, act, w_down, b_down)


def _combine_kernel(pos_ref, x_ref, gt_ref, g_ref, b_ref, eo_ref, o_ref, ob_ref, buf_ref, sem, *, alpha):
    tc = x_ref.shape[0]

    def issue(r, carry):
        for k in range(TOP_K):
            _row_copy(eo_ref, pos_ref[0, 0, r * TOP_K + k], buf_ref.at[k], r, sem).start()
        return carry

    lax.fori_loop(0, tc, issue, 0)

    def drain(r, carry):
        for k in range(TOP_K):
            _row_copy(eo_ref, 0, buf_ref.at[k], 0, sem).wait()
        return carry

    lax.fori_loop(0, tc, drain, 0)

    gt = gt_ref[...]
    y = alpha * x_ref[...]
    for k in range(TOP_K):
        y = y + gt[:, k:k + 1] * buf_ref[k]
    mu = jnp.mean(y, axis=-1, keepdims=True)
    yc = y - mu
    var = jnp.mean(yc * yc, axis=-1, keepdims=True)
    out = yc * lax.rsqrt(var + 1e-5) * g_ref[...] + b_ref[...]
    o_ref[...] = out
    ob_ref[...] = out.astype(ob_ref.dtype)


def _combine_ln(pos, x1, gates, g, b, eo, alpha, tile=256):
    n, d = x1.shape
    pos3 = pos.reshape(n // tile, 1, tile * TOP_K)
    row = lambda w: pl.BlockSpec((1, w), lambda i: (0, 0))
    return pl.pallas_call(
        functools.partial(_combine_kernel, alpha=alpha),
        grid=(n // tile,),
        in_specs=[pl.BlockSpec((1, 1, tile * TOP_K), lambda i: (i, 0, 0), memory_space=pltpu.SMEM),
                  pl.BlockSpec((tile, d), lambda i: (i, 0)),
                  pl.BlockSpec((tile, LANES), lambda i: (i, 0)),
                  row(d), row(d),
                  pl.BlockSpec(memory_space=pl.ANY)],
        out_specs=[pl.BlockSpec((tile, d), lambda i: (i, 0)),
                   pl.BlockSpec((tile, d), lambda i: (i, 0))],
        out_shape=[jax.ShapeDtypeStruct((n, d), F32), jax.ShapeDtypeStruct((n, d), BF16)],
        scratch_shapes=[pltpu.VMEM((TOP_K, tile, d), F32), pltpu.SemaphoreType.DMA(())],
        compiler_params=_cparams("arbitrary"),
        name="moe_combine_ln",
    )(pos3, x1, gates, g, b, eo)


def _slot_positions(top_i, n_tiles):
    tm = MOE_TM
    e_flat = top_i.reshape(-1)
    onehot = (e_flat[:, None] == jnp.arange(N_EXPERTS, dtype=I32)[None, :]).astype(I32)
    csum = jnp.cumsum(onehot, axis=0)
    rank = jnp.sum(csum * onehot, axis=1) - 1
    counts = csum[-1]
    padded = ((counts + tm - 1) // tm) * tm
    ends = jnp.cumsum(padded)
    pos = (ends - padded)[e_flat] + rank
    n_used = (ends[-1] // tm).astype(I32)
    tile_e = jnp.sum((ends[None, :] <= (jnp.arange(n_tiles, dtype=I32) * tm)[:, None]).astype(I32), axis=1)
    last_e = jnp.max(jnp.where(counts > 0, jnp.arange(N_EXPERTS, dtype=I32), 0))
    tile_e = jnp.minimum(tile_e, last_e)
    return pos.astype(I32), tile_e, n_used.reshape(1)


def _in_proj_columns():
    gw = GROUP_WIDTH
    gm, ssm = 0, 2 * gw
    xbc_w = gw + 4 * SSM_STATE
    ml = ssm + gw + xbc_w + SSM_HEADS
    att = ml + 4 * gw + 2 * ML_HEADS
    r = lambda a, b: list(range(a, b))
    big = (r(gm, gm + 2 * gw) + r(ssm + gw, ssm + gw + xbc_w) + r(ssm, ssm + gw) + r(ml, ml + 4 * gw)
           + r(att + Q_RANK, att + Q_RANK + 2 * gw) + r(att, att + Q_RANK))
    kidx0 = att + Q_RANK + 2 * gw
    small = {0: r(kidx0, kidx0 + IDX_DIM + IDX_HEADS),
             LANES: r(ssm + gw + xbc_w, ssm + gw + xbc_w + SSM_HEADS),
             2 * LANES: r(ml + 4 * gw, ml + 4 * gw + 2 * ML_HEADS)}
    return np.asarray(big, np.int32), small


def _pad_row(v, width, fill=0.0):
    v = v.reshape(1, -1).astype(F32)
    return jnp.pad(v, ((0, 0), (0, width - v.shape[1])), constant_values=fill)


def _rope_rows():
    def rows(rot, starts):
        half = rot // 2
        inv = ROPE_THETA ** (-jnp.arange(half, dtype=F32) * 2.0 / rot)
        f = jnp.zeros((LANES,), F32)
        s = jnp.zeros((LANES,), F32)
        for st in starts:
            f = f.at[st:st + half].set(inv).at[st + half:st + rot].set(inv)
            s = s.at[st:st + half].set(-1.0).at[st + half:st + rot].set(1.0)
        return f.reshape(1, LANES), s.reshape(1, LANES)
    fm, sm = rows(ATT_DIM // 4, [0])
    fi, si = rows(IDX_DIM // 4, [0, IDX_DIM])
    fk, sk = rows(IDX_DIM // 4, [0])
    return fm, sm, fi, si, fk, sk


def kernel(x, positions, w_in, gm_ln_g, gm_ln_b, gm_ws, gm_bs, ssm_conv_w, ssm_conv_b, ssm_dt_bias, ssm_a_log,
           ssm_d, ssm_norm_g, ml_b_i, ml_b_f, ml_norm_g, att_q_norm_g, att_w_uq, idx_k_ln_g, idx_k_ln_b, w_out,
           ln1_g, ln1_b, ln2_g, ln2_b, router_w, router_b, expert_w_gu, expert_b_gu, expert_w_down,
           expert_b_down):
    bsz, seq, d = x.shape
    depth = w_in.shape[0]
    n = bsz * seq
    alpha = (2.0 * depth) ** 0.25
    n_slots = n * TOP_K + N_EXPERTS * MOE_TM
    n_tiles = n_slots // MOE_TM
    big_cols, small_cols = _in_proj_columns()
    pos = positions.reshape(n, 1).astype(I32)

    xf = x.reshape(n, d)
    xb = xf.astype(BF16)
    for l in range(depth):
        w_big = jnp.pad(jnp.take(w_in[l], big_cols, axis=1), ((0, 0), (0, H_COLS - big_cols.size))).astype(BF16)
        w_small = jnp.zeros((d, HS_COLS), F32)
        for c0, cols in small_cols.items():
            w_small = w_small.at[:, c0:c0 + len(cols)].set(jnp.take(w_in[l], np.asarray(cols, np.int32), axis=1))
        h = _matmul(xb, w_big, BF16, 1024, 1024)
        hs = _matmul(xb, w_small.astype(BF16), F32, 1024, HS_COLS)

        y_gm = _gmlp(h, gm_ln_g[l].reshape(1, -1), gm_ln_b[l].reshape(1, -1), gm_ws[l],
                     gm_bs[l].reshape(GM_HEADS, GM_BLOCK, 1))
        y_ssm = _ssd(h, hs, bsz, seq, ssm_conv_w[l], ssm_conv_b[l].reshape(1, -1),
                     _pad_row(ssm_dt_bias[l], LANES), _pad_row(-jnp.exp(ssm_a_log[l].astype(F32)), LANES),
                     jnp.repeat(ssm_d[l].astype(F32), SSM_HEADDIM).reshape(1, -1), ssm_norm_g[l].reshape(1, -1))
        y_ml = _mlstm(h, hs, bsz, seq, _pad_row(jnp.concatenate([ml_b_i[l], ml_b_f[l]]), LANES),
                      ml_norm_g[l].reshape(1, -1))
        y_att = _dsa(h, hs, pos, bsz, seq, att_q_norm_g[l], att_w_uq[l], idx_k_ln_g[l], idx_k_ln_b[l])

        x1, top_i, gates = _mix_ln_router(
            (y_gm, y_ssm, y_ml, y_att), xf, w_out[l].astype(BF16), ln1_g[l].reshape(1, -1), ln1_b[l].reshape(1, -1),
            jnp.pad(router_w[l], ((0, 0), (0, LANES - N_EXPERTS))),
            _pad_row(router_b[l], LANES, -jnp.inf), alpha)

        slot, tile_e, n_used = _slot_positions(top_i[:, :TOP_K], n_tiles)
        xs = _dispatch(slot, x1, n_slots)
        act = _expert_up(tile_e, n_used, xs, expert_w_gu, expert_b_gu[:, :, None, :], l)
        eo = _expert_down(tile_e, n_used, act, expert_w_down, expert_b_down[:, :, None, :], l)
        xf, xb = _combine_ln(slot, x1, gates, ln2_g[l].reshape(1, -1), ln2_b[l].reshape(1, -1), eo, alpha)
    return xf.reshape(bsz, seq, d)
```

```python
import functools
import math

import numpy as np
import jax
import jax.numpy as jnp
from jax import lax
from jax.experimental import pallas as pl
from jax.experimental.pallas import tpu as pltpu

F32 = jnp.float32
BF16 = jnp.bfloat16
I32 = jnp.int32

LANES = 128
CHUNK = 64
GROUP_WIDTH = 512
GM_BLOCK = 128
GM_HEADS = 4
SSM_HEADS = 8
SSM_HEADDIM = 64
SSM_STATE = 128
SSM_CONV = 4
ML_HEADS = 4
ML_DIM = 128
ATT_HEADS = 4
ATT_DIM = 128
Q_RANK = 384
IDX_HEADS = 8
IDX_DIM = 64
IDX_TOPK_MAX = 256
ROPE_THETA = 500000.0
N_EXPERTS = 32
TOP_K = 4
SWIGLU_LIMIT = 7.0
SWIGLU_ALPHA = 1.702

SEQ_CHUNK = 128
SEQ_TILE = 512
DSA_SB = 128
DSA_QB = 512
DSA_TK = 512
DSA_TILE_GROUP = 4
MOE_TM = 512
VMEM_LIMIT = 56 * 1024 * 1024

NEG_BIG = -1e30
INT_MIN = -(2 ** 31)
INT_MAX = 2 ** 31 - 1
NEG_INF_KEY = int(np.int32(np.uint32(0xFF800000) ^ np.uint32(0x7FFFFFFF)))

H_GM, H_XBC, H_Z, H_Q, H_K, H_V, H_O, H_AK, H_AV, H_CQ, H_COLS = (
    0, 1024, 2048, 2560, 3072, 3584, 4096, 4608, 5120, 5632, 6144)
HS_COLS = 384


def _cparams(*sem):
    return pltpu.CompilerParams(dimension_semantics=sem, vmem_limit_bytes=VMEM_LIMIT)


def _dot(a, b):
    return jnp.dot(a, b, preferred_element_type=F32)


def _dot_nt(a, b):
    return lax.dot_general(a, b, (((1,), (1,)), ((), ())), preferred_element_type=F32)


def _dot_exact_lhs(tri, x):
    hi = x.astype(BF16)
    r1 = x - hi.astype(F32)
    mid = r1.astype(BF16)
    lo = (r1 - mid.astype(F32)).astype(BF16)
    return _dot(tri, hi) + _dot(tri, mid) + _dot(tri, lo)


def _lower_tri(n):
    r = lax.broadcasted_iota(I32, (n, n), 0)
    c = lax.broadcasted_iota(I32, (n, n), 1)
    return r >= c


def _sigmoid(x):
    return 1.0 / (1.0 + jnp.exp(-x))


def _softplus(x):
    return jnp.maximum(x, 0.0) + jnp.log(1.0 + jnp.exp(-jnp.abs(x)))


def _log_sigmoid(x):
    return -_softplus(-x)


def _mm_kernel(a_ref, b_ref, o_ref):
    o_ref[...] = _dot(a_ref[...], b_ref[...]).astype(o_ref.dtype)


def _matmul(a, b, out_dtype, tm, tn):
    m, k = a.shape
    n = b.shape[1]
    return pl.pallas_call(
        _mm_kernel,
        grid=(n // tn, m // tm),
        in_specs=[pl.BlockSpec((tm, k), lambda j, i: (i, 0)),
                  pl.BlockSpec((k, tn), lambda j, i: (0, j))],
        out_specs=pl.BlockSpec((tm, tn), lambda j, i: (i, j)),
        out_shape=jax.ShapeDtypeStruct((m, n), out_dtype),
        compiler_params=_cparams("parallel", "parallel"),
        name="in_proj",
    )(a, b)


def _gmlp_kernel(h_ref, lng_ref, lnb_ref, ws_ref, bs_ref, o_ref):
    t = h_ref.shape[0]
    h = h_ref[...].astype(F32)
    h = 0.5 * h * (1.0 + lax.erf(h * (1.0 / math.sqrt(2.0))))
    r = lax.broadcasted_iota(I32, (GM_BLOCK, GM_BLOCK), 0)
    c = lax.broadcasted_iota(I32, (GM_BLOCK, GM_BLOCK), 1)
    allowed = (c // CHUNK) <= (r // CHUNK)
    for g in range(GM_HEADS):
        u = h[:, g * LANES:(g + 1) * LANES]
        v = h[:, GROUP_WIDTH + g * LANES:GROUP_WIDTH + (g + 1) * LANES]
        mu = jnp.mean(v, axis=-1, keepdims=True)
        vc = v - mu
        var = jnp.mean(vc * vc, axis=-1, keepdims=True)
        vn = vc * lax.rsqrt(var + 1e-5) * lng_ref[:, g * LANES:(g + 1) * LANES] \
            + lnb_ref[:, g * LANES:(g + 1) * LANES]
        w = jnp.where(allowed, ws_ref[g], 0.0).astype(BF16)
        for wdw in range(t // GM_BLOCK):
            rows = slice(wdw * GM_BLOCK, (wdw + 1) * GM_BLOCK)
            vmix = _dot(w, vn[rows].astype(BF16)) + bs_ref[g]
            o_ref[rows, g * LANES:(g + 1) * LANES] = (u[rows] * vmix).astype(o_ref.dtype)


def _gmlp(h, lng, lnb, ws, bs, tile=256):
    n = h.shape[0]
    return pl.pallas_call(
        _gmlp_kernel,
        grid=(n // tile,),
        in_specs=[pl.BlockSpec((tile, 2 * GROUP_WIDTH), lambda i: (i, H_GM // (2 * GROUP_WIDTH))),
                  pl.BlockSpec((1, GROUP_WIDTH), lambda i: (0, 0)),
                  pl.BlockSpec((1, GROUP_WIDTH), lambda i: (0, 0)),
                  pl.BlockSpec((GM_HEADS, GM_BLOCK, GM_BLOCK), lambda i: (0, 0, 0)),
                  pl.BlockSpec((GM_HEADS, GM_BLOCK, 1), lambda i: (0, 0, 0))],
        out_specs=pl.BlockSpec((tile, GROUP_WIDTH), lambda i: (i, 0)),
        out_shape=jax.ShapeDtypeStruct((n, GROUP_WIDTH), BF16),
        compiler_params=_cparams("parallel"),
        name="gmlp",
    )(h, lng, lnb, ws, bs)


def _ssd_kernel(xbc_ref, z_ref, dt_ref, cw_ref, cb_ref, dtb_ref, a_ref, dsk_ref, ng_ref, o_ref,
                buf_ref, act_ref, st_ref):
    t = xbc_ref.shape[0]
    lc = SEQ_CHUNK

    @pl.when(pl.program_id(1) == 0)
    def _():
        buf_ref[0:8, :] = jnp.zeros((8, buf_ref.shape[1]), F32)
        st_ref[...] = jnp.zeros(st_ref.shape, F32)

    buf_ref[8:8 + t, :] = xbc_ref[...].astype(F32)
    conv = cb_ref[...] + cw_ref[0:1, :] * buf_ref[5:5 + t, :]
    for k in range(1, SSM_CONV):
        conv = conv + cw_ref[k:k + 1, :] * buf_ref[5 + k:5 + k + t, :]
    act_ref[...] = conv * _sigmoid(conv)
    buf_ref[0:8, :] = buf_ref[t:t + 8, :]

    tri = _lower_tri(lc)
    tri_b = tri.astype(BF16)
    lane = lax.broadcasted_iota(I32, (lc, LANES), 1)
    first_half = lane < SSM_HEADDIM
    a_row = a_ref[...]

    def chunk(ci, carry):
        r0 = pl.multiple_of(ci * lc, lc)
        xa = act_ref[pl.ds(r0, lc), :]
        dt = _softplus(dt_ref[pl.ds(r0, lc), :] + dtb_ref[...])
        acs = _dot_exact_lhs(tri_b, dt * a_row)
        acs_t = acs.T
        ys = []
        for p in range(SSM_HEADS // 2):
            g = p // 2
            h0, h1 = 2 * p, 2 * p + 1
            bm = xa[:, GROUP_WIDTH + g * SSM_STATE:GROUP_WIDTH + (g + 1) * SSM_STATE]
            cm = xa[:, GROUP_WIDTH + 2 * SSM_STATE + g * SSM_STATE:GROUP_WIDTH + 2 * SSM_STATE + (g + 1) * SSM_STATE]
            cm_b = cm.astype(BF16)
            cb = _dot_nt(cm_b, bm.astype(BF16))
            x2 = xa[:, p * LANES:(p + 1) * LANES]
            dt2 = jnp.where(first_half, dt[:, h0:h0 + 1], dt[:, h1:h1 + 1])
            xdt = x2 * dt2
            xdt_b = xdt.astype(BF16)
            yd = []
            for hh in (h0, h1):
                decay = jnp.where(tri, jnp.exp(acs[:, hh:hh + 1] - acs_t[hh:hh + 1, :]), 0.0)
                yd.append(_dot((cb * decay).astype(BF16), xdt_b))
            y_diag = jnp.where(first_half, yd[0], yd[1])
            acs2 = jnp.where(first_half, acs[:, h0:h0 + 1], acs[:, h1:h1 + 1])
            aend2 = jnp.where(first_half[0:1], acs[lc - 1:lc, h0:h0 + 1], acs[lc - 1:lc, h1:h1 + 1])
            st = st_ref[p]
            y_off = _dot(cm_b, st.astype(BF16)) * jnp.exp(acs2)
            ys.append(y_diag + y_off + x2 * dsk_ref[:, p * LANES:(p + 1) * LANES])
            upd = _dot(bm.T.astype(BF16), (xdt * jnp.exp(aend2 - acs2)).astype(BF16))
            st_ref[p] = jnp.exp(aend2) * st + upd
        y = jnp.concatenate(ys, axis=-1)
        zz = z_ref[pl.ds(r0, lc), :].astype(F32)
        y = y * (zz * _sigmoid(zz))
        y = y * lax.rsqrt(jnp.mean(y * y, axis=-1, keepdims=True) + 1e-6) * ng_ref[...]
        o_ref[pl.ds(r0, lc), :] = y.astype(o_ref.dtype)
        return carry

    lax.fori_loop(0, t // lc, chunk, 0)


def _ssd(h, hs, bsz, seq, cw, cb, dtb, a_row, dsk, ng, tile=SEQ_TILE):
    n = h.shape[0]
    nt = seq // tile
    xw = 2 * GROUP_WIDTH
    return pl.pallas_call(
        _ssd_kernel,
        grid=(bsz, nt),
        in_specs=[pl.BlockSpec((tile, xw), lambda b, j: (b * nt + j, H_XBC // xw)),
                  pl.BlockSpec((tile, GROUP_WIDTH), lambda b, j: (b * nt + j, H_Z // GROUP_WIDTH)),
                  pl.BlockSpec((tile, LANES), lambda b, j: (b * nt + j, 1)),
                  pl.BlockSpec((SSM_CONV, xw), lambda b, j: (0, 0)),
                  pl.BlockSpec((1, xw), lambda b, j: (0, 0)),
                  pl.BlockSpec((1, LANES), lambda b, j: (0, 0)),
                  pl.BlockSpec((1, LANES), lambda b, j: (0, 0)),
                  pl.BlockSpec((1, GROUP_WIDTH), lambda b, j: (0, 0)),
                  pl.BlockSpec((1, GROUP_WIDTH), lambda b, j: (0, 0))],
        out_specs=pl.BlockSpec((tile, GROUP_WIDTH), lambda b, j: (b * nt + j, 0)),
        out_shape=jax.ShapeDtypeStruct((n, GROUP_WIDTH), BF16),
        scratch_shapes=[pltpu.VMEM((tile + 8, xw), F32),
                        pltpu.VMEM((tile, xw), F32),
                        pltpu.VMEM((SSM_HEADS // 2, SSM_STATE, LANES), F32)],
        compiler_params=_cparams("arbitrary", "arbitrary"),
        name="ssd",
    )(h, h, hs, cw, cb, dtb, a_row, dsk, ng)


def _mlstm_kernel(q_ref, k_ref, v_ref, og_ref, gt_ref, bias_ref, ng_ref, o_ref, c_ref, n_ref, m_ref):
    t = q_ref.shape[0]
    lc = SEQ_CHUNK
    scale = ML_DIM ** -0.5

    @pl.when(pl.program_id(1) == 0)
    def _():
        c_ref[...] = jnp.zeros(c_ref.shape, F32)
        n_ref[...] = jnp.zeros(n_ref.shape, F32)
        m_ref[...] = jnp.zeros(m_ref.shape, F32)

    tri = _lower_tri(lc)
    tri_b = tri.astype(BF16)

    def chunk(ci, carry):
        r0 = pl.multiple_of(ci * lc, lc)
        gts = gt_ref[pl.ds(r0, lc), :] + bias_ref[...]
        bcum = _dot_exact_lhs(tri_b, _log_sigmoid(gts))
        gts_t = gts.T
        bcum_t = bcum.T
        for hd in range(ML_HEADS):
            cols = slice(hd * ML_DIM, (hd + 1) * ML_DIM)
            qh = q_ref[pl.ds(r0, lc), cols]
            kh = (k_ref[pl.ds(r0, lc), cols].astype(F32) * scale).astype(BF16)
            vh = v_ref[pl.ds(r0, lc), cols]
            i_col = gts[:, hd:hd + 1]
            i_row = gts_t[hd:hd + 1, :]
            b_col = bcum[:, ML_HEADS + hd:ML_HEADS + hd + 1]
            b_row = bcum_t[ML_HEADS + hd:ML_HEADS + hd + 1, :]
            b_end = b_col[lc - 1:lc, :]
            m_prev = m_ref[hd][0:1, 0:1]
            c_prev = c_ref[hd]
            n_prev = n_ref[hd][0:1, :]
            log_d = jnp.where(tri, b_col - b_row + i_row, -jnp.inf)
            inter = b_col + m_prev
            m_j = jnp.maximum(jnp.max(log_d, axis=-1, keepdims=True), inter)
            s_mat = jnp.exp(log_d - m_j) * _dot_nt(qh, kh)
            gsc = jnp.exp(inter - m_j)
            num = _dot(s_mat.astype(BF16), vh) + gsc * _dot_nt(qh, c_prev.astype(BF16))
            den = jnp.sum(s_mat, axis=-1, keepdims=True) \
                + gsc * jnp.sum(qh.astype(F32) * n_prev, axis=-1, keepdims=True)
            hh = num / jnp.maximum(jnp.abs(den), jnp.exp(-m_j))
            mu = jnp.mean(hh, axis=-1, keepdims=True)
            hc = hh - mu
            var = jnp.mean(hc * hc, axis=-1, keepdims=True)
            hn = hc * lax.rsqrt(var + 1e-5) * ng_ref[:, cols]
            og = og_ref[pl.ds(r0, lc), cols].astype(F32)
            o_ref[pl.ds(r0, lc), cols] = (hn * _sigmoid(og)).astype(o_ref.dtype)
            a_col = b_end - b_col + i_col
            a_row = b_end - b_row + i_row
            m_new = jnp.maximum(b_end + m_prev, jnp.max(a_row, axis=-1, keepdims=True))
            decay = jnp.exp(b_end + m_prev - m_new)
            w_col = jnp.exp(a_col - m_new)
            kf = kh.astype(F32)
            vw_t = (vh.astype(F32) * w_col).T.astype(BF16)
            c_ref[hd] = decay * c_prev + _dot(vw_t, kh)
            n_new = decay * n_prev + jnp.sum(w_col * kf, axis=0, keepdims=True)
            n_ref[hd] = jnp.broadcast_to(n_new, (8, ML_DIM))
            m_ref[hd] = jnp.broadcast_to(m_new, (8, LANES))
        return carry

    lax.fori_loop(0, t // lc, chunk, 0)


def _mlstm(h, hs, bsz, seq, bias, ng, tile=SEQ_TILE):
    n = h.shape[0]
    nt = seq // tile
    gw = GROUP_WIDTH

    def col(c0):
        return pl.BlockSpec((tile, gw), lambda b, j: (b * nt + j, c0 // gw))

    return pl.pallas_call(
        _mlstm_kernel,
        grid=(bsz, nt),
        in_specs=[col(H_Q), col(H_K), col(H_V), col(H_O),
                  pl.BlockSpec((tile, LANES), lambda b, j: (b * nt + j, 2)),
                  pl.BlockSpec((1, LANES), lambda b, j: (0, 0)),
                  pl.BlockSpec((1, gw), lambda b, j: (0, 0))],
        out_specs=pl.BlockSpec((tile, gw), lambda b, j: (b * nt + j, 0)),
        out_shape=jax.ShapeDtypeStruct((n, gw), BF16),
        scratch_shapes=[pltpu.VMEM((ML_HEADS, ML_DIM, ML_DIM), F32),
                        pltpu.VMEM((ML_HEADS, 8, ML_DIM), F32),
                        pltpu.VMEM((ML_HEADS, 8, LANES), F32)],
        compiler_params=_cparams("arbitrary", "arbitrary"),
        name="mlstm",
    )(h, h, h, h, hs, bias, ng)


def _att_prep_kernel(cq_ref, ak_ref, hs_ref, pos_ref, qg_ref, wuq_ref, kg_ref, kb_ref,
                     fm_ref, sm_ref, fi_ref, si_ref, fk_ref, sk_ref,
                     q_ref, k_ref, qi_ref, ki_ref, w_ref):
    t = cq_ref.shape[0]
    lane = lax.broadcasted_iota(I32, (t, LANES), 1)
    pos = pos_ref[...].astype(F32)

    cq = cq_ref[:, 0:Q_RANK].astype(F32)
    cq = cq * lax.rsqrt(jnp.mean(cq * cq, axis=-1, keepdims=True) + 1e-6) * qg_ref[...]
    q_all = _dot(cq.astype(BF16), wuq_ref[...])

    def rope(x, cos, sin_signed, half, period):
        first = (lane % period) < half
        partner = jnp.where(first, pltpu.roll(x, LANES - half, 1), pltpu.roll(x, half, 1))
        return x * cos + partner * sin_signed

    ang = pos * fm_ref[...]
    cos_m, sin_m = jnp.cos(ang), jnp.sin(ang) * sm_ref[...]
    half_m = ATT_DIM // 8
    for hd in range(ATT_HEADS):
        cols = slice(hd * ATT_DIM, (hd + 1) * ATT_DIM)
        q_ref[:, cols] = (rope(q_all[:, cols], cos_m, sin_m, half_m, LANES) * (ATT_DIM ** -0.5)).astype(q_ref.dtype)
        k_ref[:, cols] = rope(ak_ref[:, cols].astype(F32), cos_m, sin_m, half_m, LANES).astype(k_ref.dtype)

    ang = pos * fi_ref[...]
    cos_i, sin_i = jnp.cos(ang), jnp.sin(ang) * si_ref[...]
    half_i = IDX_DIM // 8
    low = lane < IDX_DIM
    for pr in range(IDX_HEADS // 2):
        x = rope(q_all[:, GROUP_WIDTH + pr * LANES:GROUP_WIDTH + (pr + 1) * LANES], cos_i, sin_i, half_i, IDX_DIM)
        even = jnp.where(low, x, 0.0).astype(qi_ref.dtype)
        odd = jnp.where(low, pltpu.roll(x, IDX_DIM, 1), 0.0).astype(qi_ref.dtype)
        for blk in range(t // DSA_SB):
            rows = slice(blk * DSA_SB, (blk + 1) * DSA_SB)
            qi_ref[blk, 2 * pr] = even[rows]
            qi_ref[blk, 2 * pr + 1] = odd[rows]

    hs = hs_ref[...]
    kx = jnp.where(low, hs, 0.0)
    mu = jnp.sum(kx, axis=-1, keepdims=True) * (1.0 / IDX_DIM)
    kc = jnp.where(low, hs - mu, 0.0)
    var = jnp.sum(kc * kc, axis=-1, keepdims=True) * (1.0 / IDX_DIM)
    kn = kc * lax.rsqrt(var + 1e-5) * kg_ref[...] + kb_ref[...]
    ang = pos * fk_ref[...]
    ki_ref[...] = rope(kn, jnp.cos(ang), jnp.sin(ang) * sk_ref[...], half_i, LANES).astype(ki_ref.dtype)
    wi = jnp.where(lane < IDX_HEADS, pltpu.roll(hs, LANES - IDX_DIM, 1), 0.0) * (IDX_HEADS ** -0.5 * IDX_DIM ** -0.5)
    for blk in range(t // DSA_SB):
        w_ref[blk] = wi[blk * DSA_SB:(blk + 1) * DSA_SB, :].T[0:IDX_HEADS, :]


def _att_prep(h, hs, pos, qg, wuq, kg, kb, rope_rows, tile=256):
    n = h.shape[0]
    gw = GROUP_WIDTH
    row = lambda w: pl.BlockSpec((1, w), lambda i: (0, 0))
    return pl.pallas_call(
        _att_prep_kernel,
        grid=(n // tile,),
        in_specs=[pl.BlockSpec((tile, gw), lambda i: (i, H_CQ // gw)),
                  pl.BlockSpec((tile, gw), lambda i: (i, H_AK // gw)),
                  pl.BlockSpec((tile, LANES), lambda i: (i, 0)),
                  pl.BlockSpec((tile, 1), lambda i: (i, 0)),
                  row(Q_RANK),
                  pl.BlockSpec(wuq.shape, lambda i: (0, 0)),
                  row(LANES), row(LANES)] + [row(LANES)] * 6,
        out_specs=[pl.BlockSpec((tile, gw), lambda i: (i, 0)),
                   pl.BlockSpec((tile, gw), lambda i: (i, 0)),
                   pl.BlockSpec((tile // DSA_SB, IDX_HEADS, DSA_SB, LANES), lambda i: (i, 0, 0, 0)),
                   pl.BlockSpec((tile, LANES), lambda i: (i, 0)),
                   pl.BlockSpec((tile // DSA_SB, IDX_HEADS, DSA_SB), lambda i: (i, 0, 0))],
        out_shape=[jax.ShapeDtypeStruct((n, gw), BF16),
                   jax.ShapeDtypeStruct((n, gw), BF16),
                   jax.ShapeDtypeStruct((n // DSA_SB, IDX_HEADS, DSA_SB, LANES), BF16),
                   jax.ShapeDtypeStruct((n, LANES), BF16),
                   jax.ShapeDtypeStruct((n // DSA_SB, IDX_HEADS, DSA_SB), F32)],
        compiler_params=_cparams("parallel"),
        name="att_prep",
    )(h, h, hs, pos, qg, wuq, kg, kb, *rope_rows)


def _dsa_select_kernel(qi_ref, w_ref, ki_ref, bias_ref, key_ref, hi_ref, lo_ref, tri_ref, *, top_k):
    sb = w_ref.shape[2]
    tk = key_ref.shape[1]
    nt_all = bias_ref.shape[1]
    i = pl.program_id(1)
    n_tiles = (i * sb + sb + tk - 1) // tk
    row = lax.broadcasted_iota(I32, (1, sb), 1)
    lim = ((i * sb + row) // CHUNK + 1) * CHUNK
    sub_t = lax.broadcasted_iota(I32, (tk, sb), 0)
    w = w_ref[0]
    half_min = -(2 ** 15)

    def fill(kt, carry):
        k0 = pl.multiple_of(kt * tk, tk)
        kk = ki_ref[pl.ds(k0, tk), :]
        acc = None
        for pr in range(IDX_HEADS // 2):
            rel = _dot_nt(kk, qi_ref[2 * pr * sb:(2 * pr + 2) * sb, :])
            for u in range(2):
                term = jnp.maximum(rel[:, u * sb:(u + 1) * sb], 0.0) * w[2 * pr + u:2 * pr + u + 1, :]
                acc = term if acc is None else acc + term
        score = jnp.where(sub_t + k0 < lim, acc + 0.0, -jnp.inf)
        bits = pltpu.bitcast(score, I32)
        keys = bits ^ ((bits >> 31) & INT_MAX)
        key_ref[kt] = keys
        hi_ref[kt] = (keys >> 16).astype(jnp.int16)
        return carry

    lax.fori_loop(0, n_tiles, fill, 0)

    group = DSA_TILE_GROUP
    n_groups = (n_tiles + group - 1) // group

    def pad_tile(kt, carry):
        key_ref[kt] = jnp.full((tk, sb), NEG_INF_KEY, I32)
        hi_ref[kt] = jnp.full((tk, sb), NEG_INF_KEY >> 16, jnp.int16)
        return carry

    lax.fori_loop(n_tiles, n_groups * group, pad_tile, 0)

    def tree_sum(parts):
        while len(parts) > 1:
            parts = [parts[j] + parts[j + 1] for j in range(0, len(parts), 2)]
        return parts[0]

    def count(ref, pred, dtype):
        gran = 32 // jnp.dtype(dtype).itemsize

        def body(g, acc):
            for u in range(group):
                kt = g * group + u
                m = pred(ref[kt], kt).astype(dtype)
                acc = acc + tree_sum([m[c * gran:(c + 1) * gran] for c in range(tk // gran)])
            return acc
        acc = lax.fori_loop(0, n_groups, body, jnp.zeros((gran, sb), dtype))
        return jnp.sum(acc.astype(I32), axis=0, keepdims=True)

    def count16_ge3(ref, cands):
        c16 = [c.astype(jnp.int16) for c in cands]

        def body(g, accs):
            accs = list(accs)
            for u in range(group):
                v = ref[g * group + u]
                for j in range(3):
                    m = (v >= c16[j]).astype(jnp.int16)
                    accs[j] = accs[j] + tree_sum([m[c * 16:(c + 1) * 16] for c in range(tk // 16)])
            return tuple(accs)

        zero = jnp.zeros((16, sb), jnp.int16)
        accs = lax.fori_loop(0, n_groups, body, (zero, zero, zero))
        return [jnp.sum(a.astype(I32), axis=0, keepdims=True) for a in accs]

    def greedy16(ref, need):
        def step(s, t):
            unit = lax.shift_left(jnp.int32(1), 14 - 2 * s)
            counts = count16_ge3(ref, [t + unit, t + 2 * unit, t + 3 * unit])
            taken = sum((cnt >= need).astype(I32) for cnt in counts)
            return t + taken * unit
        return lax.fori_loop(0, 8, step, jnp.full((1, sb), half_min, I32))

    t_hi = greedy16(hi_ref, top_k)
    t_hi16 = t_hi.astype(jnp.int16)
    above = count(hi_ref, lambda v, kt: v > t_hi16, jnp.int16)
    need_lo = top_k - above

    def fill_lo(kt, carry):
        lo = ((key_ref[kt] & 0xFFFF) + half_min).astype(jnp.int16)
        lo_ref[kt] = jnp.where(hi_ref[kt] == t_hi16, lo, jnp.int16(half_min))
        return carry

    lax.fori_loop(0, n_groups * group, fill_lo, 0)
    t_lo = greedy16(lo_ref, need_lo)
    thr = (t_hi << 16) | ((t_lo - half_min) & 0xFFFF)
    t_lo16 = t_lo.astype(jnp.int16)
    need_tied = (need_lo - count(lo_ref, lambda v, kt: v > t_lo16, jnp.int16)).astype(F32)

    r = lax.broadcasted_iota(I32, (tk, tk), 0)
    c = lax.broadcasted_iota(I32, (tk, tk), 1)
    tri_ref[...] = (r >= c).astype(tri_ref.dtype)

    def emit(kt, seen):
        keys = key_ref[kt]
        tied = keys == thr
        seen = seen + _dot(tri_ref[...], tied.astype(tri_ref.dtype))
        sel = ((keys > thr) | (tied & (seen <= need_tied))) & (sub_t + kt * tk < lim)
        bias_t = jnp.where(sel, 0.0, -jnp.inf)
        for cb in range(tk // LANES):
            bias_ref[0, kt, :, cb * LANES:(cb + 1) * LANES] = \
                bias_t[cb * LANES:(cb + 1) * LANES, :].T.astype(bias_ref.dtype)
        return seen[tk - 1:tk, :]

    lax.fori_loop(0, n_tiles, emit, jnp.zeros((1, sb), F32))

    def blank(kt, carry):
        bias_ref[0, kt] = jnp.full(bias_ref.shape[2:], -jnp.inf, bias_ref.dtype)
        return carry

    lax.fori_loop(n_tiles, nt_all, blank, 0)


def _dsa_select(qi, w, ki, bsz, seq, top_k):
    nb, _, sb = w.shape
    tk = DSA_TK
    nq, nt = seq // sb, seq // tk
    nt_pad = -(-nt // DSA_TILE_GROUP) * DSA_TILE_GROUP
    kern = functools.partial(_dsa_select_kernel, top_k=top_k)
    return pl.pallas_call(
        kern,
        grid=(bsz, nq),
        in_specs=[pl.BlockSpec((IDX_HEADS * sb, LANES), lambda b, i: (b * nq + i, 0)),
                  pl.BlockSpec((1, IDX_HEADS, sb), lambda b, i: (b * nq + i, 0, 0)),
                  pl.BlockSpec((seq, LANES), lambda b, i: (b, 0))],
        out_specs=pl.BlockSpec((1, nt, sb, tk), lambda b, i: (b * nq + i, 0, 0, 0)),
        out_shape=jax.ShapeDtypeStruct((nb, nt, sb, tk), BF16),
        scratch_shapes=[pltpu.VMEM((nt_pad, tk, sb), I32),
                        pltpu.VMEM((nt_pad, tk, sb), jnp.int16),
                        pltpu.VMEM((nt_pad, tk, sb), jnp.int16),
                        pltpu.VMEM((tk, tk), BF16)],
        compiler_params=_cparams("parallel", "parallel"),
        name="dsa_select",
    )(qi.reshape(nb * IDX_HEADS * sb, LANES), w, ki)


def _dsa_attn_kernel(qt_ref, kt_ref, first_ref, last_ref, q_ref, bias_ref, k_ref, v_ref, o_ref,
                     m_ref, l_ref, acc_ref):
    s = pl.program_id(1)
    qb, tk = q_ref.shape[0], k_ref.shape[0]

    @pl.when(first_ref[s] == 1)
    def _():
        m_ref[...] = jnp.full(m_ref.shape, NEG_BIG, F32)
        l_ref[...] = jnp.zeros(l_ref.shape, F32)
        acc_ref[...] = jnp.zeros(acc_ref.shape, F32)

    bias = bias_ref[...].reshape(qb, tk).astype(F32)

    for hd in range(ATT_HEADS):
        cols = slice(hd * ATT_DIM, (hd + 1) * ATT_DIM)
        logits = _dot_nt(q_ref[:, cols], k_ref[:, cols]) + bias
        m_prev = m_ref[:, cols]
        m_new = jnp.maximum(m_prev, jnp.max(logits, axis=-1, keepdims=True))
        p = jnp.exp(logits - m_new[:, 0:1])
        alpha = jnp.exp(m_prev - m_new)
        l_ref[:, cols] = alpha * l_ref[:, cols] + jnp.sum(p, axis=-1, keepdims=True)
        acc_ref[:, cols] = alpha * acc_ref[:, cols] + _dot(p.astype(BF16), v_ref[:, cols])
        m_ref[:, cols] = m_new

    @pl.when(last_ref[s] == 1)
    def _():
        o_ref[...] = (acc_ref[...] / l_ref[...]).astype(o_ref.dtype)


def _dsa_attention(q, bias, k, h, bsz, seq):
    n = q.shape[0]
    qb, tk = DSA_QB, DSA_TK
    nq, nkt = seq // qb, seq // tk
    sub = qb // DSA_SB
    pairs = [(i, t) for i in range(nq) for t in range((i * qb + qb - 1) // tk + 1)]
    qt = jnp.asarray([p[0] for p in pairs], I32)
    ktab = jnp.asarray([p[1] for p in pairs], I32)
    first = jnp.asarray([1 if p[1] == 0 else 0 for p in pairs], I32)
    last = jnp.asarray([1 if p[1] == (p[0] * qb + qb - 1) // tk else 0 for p in pairs], I32)
    gw = GROUP_WIDTH

    def qmap(w_):
        return pl.BlockSpec((qb, w_), lambda b, s, qt, kt, f, l: (b * nq + qt[s], 0))

    def kmap(w_, cb):
        return pl.BlockSpec((tk, w_), lambda b, s, qt, kt, f, l: (b * nkt + kt[s], cb))

    grid_spec = pltpu.PrefetchScalarGridSpec(
        num_scalar_prefetch=4,
        grid=(bsz, len(pairs)),
        in_specs=[qmap(gw),
                  pl.BlockSpec((sub, 1, DSA_SB, tk), lambda b, s, qt, kt, f, l: (b * nq + qt[s], kt[s], 0, 0)),
                  kmap(gw, 0), kmap(gw, H_AV // gw)],
        out_specs=qmap(gw),
        scratch_shapes=[pltpu.VMEM((qb, gw), F32), pltpu.VMEM((qb, gw), F32), pltpu.VMEM((qb, gw), F32)],
    )
    return pl.pallas_call(
        _dsa_attn_kernel,
        grid_spec=grid_spec,
        out_shape=jax.ShapeDtypeStruct((n, gw), BF16),
        compiler_params=_cparams("arbitrary", "arbitrary"),
        name="dsa_attention",
    )(qt, ktab, first, last, q, bias, k, h)


def _dsa(h, hs, pos, bsz, seq, q_norm_g, w_uq, idxk_g, idxk_b):
    q, k, qi, ki, wi = _att_prep(h, hs, pos, q_norm_g.reshape(1, -1), w_uq.astype(BF16),
                                 _pad_row(idxk_g, LANES), _pad_row(idxk_b, LANES), _rope_rows())
    bias = _dsa_select(qi, wi, ki, bsz, seq, min(IDX_TOPK_MAX, seq // 4))
    return _dsa_attention(q, bias, k, h, bsz, seq)


def _mix_kernel(ya_ref, yb_ref, yc_ref, yd_ref, x_ref, wo_ref, g_ref, b_ref, rw_ref, rb_ref,
                x1_ref, ti_ref, gt_ref, *, alpha):
    gw = GROUP_WIDTH
    mix = _dot(ya_ref[...], wo_ref[0:gw, :])
    mix = mix + _dot(yb_ref[...], wo_ref[gw:2 * gw, :])
    mix = mix + _dot(yc_ref[...], wo_ref[2 * gw:3 * gw, :])
    mix = mix + _dot(yd_ref[...], wo_ref[3 * gw:4 * gw, :])
    y = alpha * x_ref[...] + mix
    mu = jnp.mean(y, axis=-1, keepdims=True)
    yc = y - mu
    var = jnp.mean(yc * yc, axis=-1, keepdims=True)
    x1 = yc * lax.rsqrt(var + 1e-5) * g_ref[...] + b_ref[...]
    x1_ref[...] = x1

    logits = jnp.dot(x1, rw_ref[...], precision=lax.Precision.HIGHEST, preferred_element_type=F32) + rb_ref[...]
    lane = lax.broadcasted_iota(I32, logits.shape, 1)
    vals, idxs = [], []
    for _ in range(TOP_K):
        mx = jnp.max(logits, axis=-1, keepdims=True)
        ix = jnp.min(jnp.where(logits == mx, lane, LANES), axis=-1, keepdims=True)
        vals.append(mx)
        idxs.append(ix)
        logits = jnp.where(lane == ix, -jnp.inf, logits)
    es = [jnp.exp(v - vals[0]) for v in vals]
    tot = es[0] + es[1] + es[2] + es[3]
    ti = jnp.zeros(logits.shape, I32)
    gt = jnp.zeros(logits.shape, F32)
    for k in range(TOP_K):
        ti = jnp.where(lane == k, idxs[k], ti)
        gt = jnp.where(lane == k, es[k] / tot, gt)
    ti_ref[...] = ti
    gt_ref[...] = gt


def _mix_ln_router(ys, x, wo, g, b, rw, rb, alpha, tile=256):
    n, d = x.shape
    gw = GROUP_WIDTH
    yspec = pl.BlockSpec((tile, gw), lambda i: (i, 0))
    row = lambda w: pl.BlockSpec((1, w), lambda i: (0, 0))
    return pl.pallas_call(
        functools.partial(_mix_kernel, alpha=alpha),
        grid=(n // tile,),
        in_specs=[yspec, yspec, yspec, yspec,
                  pl.BlockSpec((tile, d), lambda i: (i, 0)),
                  pl.BlockSpec(wo.shape, lambda i: (0, 0)),
                  row(d), row(d),
                  pl.BlockSpec(rw.shape, lambda i: (0, 0)),
                  row(LANES)],
        out_specs=[pl.BlockSpec((tile, d), lambda i: (i, 0)),
                   pl.BlockSpec((tile, LANES), lambda i: (i, 0)),
                   pl.BlockSpec((tile, LANES), lambda i: (i, 0))],
        out_shape=[jax.ShapeDtypeStruct((n, d), F32),
                   jax.ShapeDtypeStruct((n, LANES), I32),
                   jax.ShapeDtypeStruct((n, LANES), F32)],
        compiler_params=_cparams("parallel"),
        name="mix_ln_router",
    )(*ys, x, wo, g, b, rw, rb)


def _row_copy(src, s, dst, d, sem):
    return pltpu.make_async_copy(src.at[pl.ds(s, 1), :], dst.at[pl.ds(d, 1), :], sem)


def _dispatch_kernel(pos_ref, x_ref, xs_in_ref, xs_ref, sem):
    del xs_in_ref
    tc = x_ref.shape[0]

    def issue(r, carry):
        for k in range(TOP_K):
            _row_copy(x_ref, r, xs_ref, pos_ref[0, 0, r * TOP_K + k], sem).start()
        return carry

    lax.fori_loop(0, tc, issue, 0)

    def drain(r, carry):
        for k in range(TOP_K):
            _row_copy(x_ref, 0, xs_ref, 0, sem).wait()
        return carry

    lax.fori_loop(0, tc, drain, 0)


def _dispatch(pos, x1, n_slots, tile=256):
    n, d = x1.shape
    pos3 = pos.reshape(n // tile, 1, tile * TOP_K)
    xs0 = jnp.zeros((n_slots, d), F32)
    return pl.pallas_call(
        _dispatch_kernel,
        grid=(n // tile,),
        in_specs=[pl.BlockSpec((1, 1, tile * TOP_K), lambda i: (i, 0, 0), memory_space=pltpu.SMEM),
                  pl.BlockSpec((tile, d), lambda i: (i, 0)),
                  pl.BlockSpec(memory_space=pl.ANY)],
        out_specs=pl.BlockSpec(memory_space=pl.ANY),
        out_shape=jax.ShapeDtypeStruct((n_slots, d), F32),
        scratch_shapes=[pltpu.SemaphoreType.DMA(())],
        input_output_aliases={2: 0},
        compiler_params=_cparams("arbitrary"),
        name="moe_dispatch",
    )(pos3, x1, xs0)


def _new_expert(te_ref, i):
    return (i == 0) | (te_ref[i] != te_ref[jnp.maximum(i - 1, 0)])


def _expert_up_kernel(te_ref, nu_ref, x_ref, wg_ref, wu_ref, bg_ref, bu_ref, o_ref, wgb_ref, wub_ref):
    i = pl.program_id(1)

    @pl.when(_new_expert(te_ref, i))
    def _():
        wgb_ref[...] = wg_ref[...].astype(BF16)
        wub_ref[...] = wu_ref[...].astype(BF16)

    @pl.when(i < nu_ref[0])
    def _():
        xb = x_ref[...].astype(BF16)
        g = _dot(xb, wgb_ref[...]) + bg_ref[...]
        u = _dot(xb, wub_ref[...]) + bu_ref[...]
        g = jnp.minimum(g, SWIGLU_LIMIT)
        u = jnp.clip(u, -SWIGLU_LIMIT, SWIGLU_LIMIT)
        o_ref[...] = ((u + 1.0) * g * _sigmoid(SWIGLU_ALPHA * g)).astype(o_ref.dtype)


def _expert_up(te, nu, xs, w_gu, b_gu, layer, splits=4):
    p, d = xs.shape
    ff = w_gu.shape[3] // 2
    fj = ff // splits
    tm = MOE_TM
    tile_of = lambda i, nu: jnp.minimum(i, nu[0] - 1)
    grid_spec = pltpu.PrefetchScalarGridSpec(
        num_scalar_prefetch=2,
        grid=(splits, p // tm),
        in_specs=[pl.BlockSpec((tm, d), lambda j, i, te, nu: (tile_of(i, nu), 0)),
                  pl.BlockSpec((None, None, d, fj), lambda j, i, te, nu: (layer, te[i], 0, j)),
                  pl.BlockSpec((None, None, d, fj), lambda j, i, te, nu: (layer, te[i], 0, splits + j)),
                  pl.BlockSpec((None, None, 1, fj), lambda j, i, te, nu: (layer, te[i], 0, j)),
                  pl.BlockSpec((None, None, 1, fj), lambda j, i, te, nu: (layer, te[i], 0, splits + j))],
        out_specs=pl.BlockSpec((tm, fj), lambda j, i, te, nu: (tile_of(i, nu), j)),
        scratch_shapes=[pltpu.VMEM((d, fj), BF16), pltpu.VMEM((d, fj), BF16)],
    )
    return pl.pallas_call(
        _expert_up_kernel,
        grid_spec=grid_spec,
        out_shape=jax.ShapeDtypeStruct((p, ff), BF16),
        compiler_params=_cparams("arbitrary", "arbitrary"),
        name="expert_up",
    )(te, nu, xs, w_gu, w_gu, b_gu, b_gu)


def _expert_down_kernel(te_ref, nu_ref, a_ref, wd_ref, bd_ref, o_ref, wdb_ref):
    i = pl.program_id(1)

    @pl.when(_new_expert(te_ref, i))
    def _():
        wdb_ref[...] = wd_ref[...].astype(BF16)

    @pl.when(i < nu_ref[0])
    def _():
        o_ref[...] = _dot(a_ref[...], wdb_ref[...]) + bd_ref[...]


def _expert_down(te, nu, act, w_down, b_down, layer, splits=2):
    p, ff = act.shape
    d = w_down.shape[3]
    dj = d // splits
    tm = MOE_TM
    tile_of = lambda i, nu: jnp.minimum(i, nu[0] - 1)
    grid_spec = pltpu.PrefetchScalarGridSpec(
        num_scalar_prefetch=2,
        grid=(splits, p // tm),
        in_specs=[pl.BlockSpec((tm, ff), lambda j, i, te, nu: (tile_of(i, nu), 0)),
                  pl.BlockSpec((None, None, ff, dj), lambda j, i, te, nu: (layer, te[i], 0, j)),
                  pl.BlockSpec((None, None, 1, dj), lambda j, i, te, nu: (layer, te[i], 0, j))],
        out_specs=pl.BlockSpec((tm, dj), lambda j, i, te, nu: (tile_of(i, nu), j)),
        scratch_shapes=[pltpu.VMEM((ff, dj), BF16)],
    )
    return pl.pallas_call(
        _expert_down_kernel,
        grid_spec=grid_spec,
        out_shape=jax.ShapeDtypeStruct((p, d), F32),
        compiler_params=_cparams("arbitrary", "arbitrary"),
        name="expert_down",
    )(te, nu, act, w_down, b_down)


def _combine_kernel(pos_ref, x_ref, gt_ref, g_ref, b_ref, eo_ref, o_ref, ob_ref, buf_ref, sem, *, alpha):
    tc = x_ref.shape[0]

    def issue(r, carry):
        for k in range(TOP_K):
            _row_copy(eo_ref, pos_ref[0, 0, r * TOP_K + k], buf_ref.at[k], r, sem).start()
        return carry

    lax.fori_loop(0, tc, issue, 0)

    def drain(r, carry):
        for k in range(TOP_K):
            _row_copy(eo_ref, 0, buf_ref.at[k], 0, sem).wait()
        return carry

    lax.fori_loop(0, tc, drain, 0)

    gt = gt_ref[...]
    y = alpha * x_ref[...]
    for k in range(TOP_K):
        y = y + gt[:, k:k + 1] * buf_ref[k]
    mu = jnp.mean(y, axis=-1, keepdims=True)
    yc = y - mu
    var = jnp.mean(yc * yc, axis=-1, keepdims=True)
    out = yc * lax.rsqrt(var + 1e-5) * g_ref[...] + b_ref[...]
    o_ref[...] = out
    ob_ref[...] = out.astype(ob_ref.dtype)


def _combine_ln(pos, x1, gates, g, b, eo, alpha, tile=256):
    n, d = x1.shape
    pos3 = pos.reshape(n // tile, 1, tile * TOP_K)
    row = lambda w: pl.BlockSpec((1, w), lambda i: (0, 0))
    return pl.pallas_call(
        functools.partial(_combine_kernel, alpha=alpha),
        grid=(n // tile,),
        in_specs=[pl.BlockSpec((1, 1, tile * TOP_K), lambda i: (i, 0, 0), memory_space=pltpu.SMEM),
                  pl.BlockSpec((tile, d), lambda i: (i, 0)),
                  pl.BlockSpec((tile, LANES), lambda i: (i, 0)),
                  row(d), row(d),
                  pl.BlockSpec(memory_space=pl.ANY)],
        out_specs=[pl.BlockSpec((tile, d), lambda i: (i, 0)),
                   pl.BlockSpec((tile, d), lambda i: (i, 0))],
        out_shape=[jax.ShapeDtypeStruct((n, d), F32), jax.ShapeDtypeStruct((n, d), BF16)],
        scratch_shapes=[pltpu.VMEM((TOP_K, tile, d), F32), pltpu.SemaphoreType.DMA(())],
        compiler_params=_cparams("arbitrary"),
        name="moe_combine_ln",
    )(pos3, x1, gates, g, b, eo)


def _slot_positions(top_i, n_tiles):
    tm = MOE_TM
    e_flat = top_i.reshape(-1)
    onehot = (e_flat[:, None] == jnp.arange(N_EXPERTS, dtype=I32)[None, :]).astype(I32)
    csum = jnp.cumsum(onehot, axis=0)
    rank = jnp.sum(csum * onehot, axis=1) - 1
    counts = csum[-1]
    padded = ((counts + tm - 1) // tm) * tm
    ends = jnp.cumsum(padded)
    pos = (ends - padded)[e_flat] + rank
    n_used = (ends[-1] // tm).astype(I32)
    tile_e = jnp.sum((ends[None, :] <= (jnp.arange(n_tiles, dtype=I32) * tm)[:, None]).astype(I32), axis=1)
    last_e = jnp.max(jnp.where(counts > 0, jnp.arange(N_EXPERTS, dtype=I32), 0))
    tile_e = jnp.minimum(tile_e, last_e)
    return pos.astype(I32), tile_e, n_used.reshape(1)


def _in_proj_columns():
    gw = GROUP_WIDTH
    gm, ssm = 0, 2 * gw
    xbc_w = gw + 4 * SSM_STATE
    ml = ssm + gw + xbc_w + SSM_HEADS
    att = ml + 4 * gw + 2 * ML_HEADS
    r = lambda a, b: list(range(a, b))
    big = (r(gm, gm + 2 * gw) + r(ssm + gw, ssm + gw + xbc_w) + r(ssm, ssm + gw) + r(ml, ml + 4 * gw)
           + r(att + Q_RANK, att + Q_RANK + 2 * gw) + r(att, att + Q_RANK))
    kidx0 = att + Q_RANK + 2 * gw
    small = {0: r(kidx0, kidx0 + IDX_DIM + IDX_HEADS),
             LANES: r(ssm + gw + xbc_w, ssm + gw + xbc_w + SSM_HEADS),
             2 * LANES: r(ml + 4 * gw, ml + 4 * gw + 2 * ML_HEADS)}
    return np.asarray(big, np.int32), small


def _pad_row(v, width, fill=0.0):
    v = v.reshape(1, -1).astype(F32)
    return jnp.pad(v, ((0, 0), (0, width - v.shape[1])), constant_values=fill)


def _rope_rows():
    def rows(rot, starts):
        half = rot // 2
        inv = ROPE_THETA ** (-jnp.arange(half, dtype=F32) * 2.0 / rot)
        f = jnp.zeros((LANES,), F32)
        s = jnp.zeros((LANES,), F32)
        for st in starts:
            f = f.at[st:st + half].set(inv).at[st + half:st + rot].set(inv)
            s = s.at[st:st + half].set(-1.0).at[st + half:st + rot].set(1.0)
        return f.reshape(1, LANES), s.reshape(1, LANES)
    fm, sm = rows(ATT_DIM // 4, [0])
    fi, si = rows(IDX_DIM // 4, [0, IDX_DIM])
    fk, sk = rows(IDX_DIM // 4, [0])
    return fm, sm, fi, si, fk, sk


def kernel(x, positions, w_in, gm_ln_g, gm_ln_b, gm_ws, gm_bs, ssm_conv_w, ssm_conv_b, ssm_dt_bias, ssm_a_log,
           ssm_d, ssm_norm_g, ml_b_i, ml_b_f, ml_norm_g, att_q_norm_g, att_w_uq, idx_k_ln_g, idx_k_ln_b, w_out,
           ln1_g, ln1_b, ln2_g, ln2_b, router_w, router_b, expert_w_gu, expert_b_gu, expert_w_down,
           expert_b_down):
    bsz, seq, d = x.shape
    depth = w_in.shape[0]
    n = bsz * seq
    alpha = (2.0 * depth) ** 0.25
    n_slots = n * TOP_K + N_EXPERTS * MOE_TM
    n_tiles = n_slots // MOE_TM
    big_cols, small_cols = _in_proj_columns()
    pos = positions.reshape(n, 1).astype(I32)

    xf = x.reshape(n, d)
    xb = xf.astype(BF16)
    for l in range(depth):
        w_big = jnp.pad(jnp.take(w_in[l], big_cols, axis=1), ((0, 0), (0, H_COLS - big_cols.size))).astype(BF16)
        w_small = jnp.zeros((d, HS_COLS), F32)
        for c0, cols in small_cols.items():
            w_small = w_small.at[:, c0:c0 + len(cols)].set(jnp.take(w_in[l], np.asarray(cols, np.int32), axis=1))
        h = _matmul(xb, w_big, BF16, 1024, 1024)
        hs = _matmul(xb, w_small.astype(BF16), F32, 1024, HS_COLS)

        y_gm = _gmlp(h, gm_ln_g[l].reshape(1, -1), gm_ln_b[l].reshape(1, -1), gm_ws[l],
                     gm_bs[l].reshape(GM_HEADS, GM_BLOCK, 1))
        y_ssm = _ssd(h, hs, bsz, seq, ssm_conv_w[l], ssm_conv_b[l].reshape(1, -1),
                     _pad_row(ssm_dt_bias[l], LANES), _pad_row(-jnp.exp(ssm_a_log[l].astype(F32)), LANES),
                     jnp.repeat(ssm_d[l].astype(F32), SSM_HEADDIM).reshape(1, -1), ssm_norm_g[l].reshape(1, -1))
        y_ml = _mlstm(h, hs, bsz, seq, _pad_row(jnp.concatenate([ml_b_i[l], ml_b_f[l]]), LANES),
                      ml_norm_g[l].reshape(1, -1))
        y_att = _dsa(h, hs, pos, bsz, seq, att_q_norm_g[l], att_w_uq[l], idx_k_ln_g[l], idx_k_ln_b[l])

        x1, top_i, gates = _mix_ln_router(
            (y_gm, y_ssm, y_ml, y_att), xf, w_out[l].astype(BF16), ln1_g[l].reshape(1, -1), ln1_b[l].reshape(1, -1),
            jnp.pad(router_w[l], ((0, 0), (0, LANES - N_EXPERTS))),
            _pad_row(router_b[l], LANES, -jnp.inf), alpha)

        slot, tile_e, n_used = _slot_positions(top_i[:, :TOP_K], n_tiles)
        xs = _dispatch(slot, x1, n_slots)
        act = _expert_up(tile_e, n_used, xs, expert_w_gu, expert_b_gu[:, :, None, :], l)
        eo = _expert_down(tile_e, n_used, act, expert_w_down, expert_b_down[:, :, None, :], l)
        xf, xb = _combine_ln(slot, x1, gates, ln2_g[l].reshape(1, -1), ln2_b[l].reshape(1, -1), eo, alpha)
    return xf.reshape(bsz, seq, d)
```

```python
import functools
import math

import numpy as np
import jax
import jax.numpy as jnp
from jax import lax
from jax.experimental import pallas as pl
from jax.experimental.pallas import tpu as pltpu

F32 = jnp.float32
BF16 = jnp.bfloat16
I32 = jnp.int32

LANES = 128
CHUNK = 64
GROUP_WIDTH = 512
GM_BLOCK = 128
GM_HEADS = 4
SSM_HEADS = 8
SSM_HEADDIM = 64
SSM_STATE = 128
SSM_CONV = 4
ML_HEADS = 4
ML_DIM = 128
ATT_HEADS = 4
ATT_DIM = 128
Q_RANK = 384
IDX_HEADS = 8
IDX_DIM = 64
IDX_TOPK_MAX = 256
ROPE_THETA = 500000.0
N_EXPERTS = 32
TOP_K = 4
SWIGLU_LIMIT = 7.0
SWIGLU_ALPHA = 1.702

SEQ_CHUNK = 128
SEQ_TILE = 512
DSA_SB = 128
DSA_QB = 512
DSA_TK = 512
DSA_TILE_GROUP = 4
MOE_TM = 512
VMEM_LIMIT = 56 * 1024 * 1024

NEG_BIG = -1e30
INT_MIN = -(2 ** 31)
INT_MAX = 2 ** 31 - 1
NEG_INF_KEY = int(np.int32(np.uint32(0xFF800000) ^ np.uint32(0x7FFFFFFF)))

H_GM, H_XBC, H_Z, H_Q, H_K, H_V, H_O, H_AK, H_AV, H_CQ, H_COLS = (
    0, 1024, 2048, 2560, 3072, 3584, 4096, 4608, 5120, 5632, 6144)
HS_COLS = 384


def _cparams(*sem):
    return pltpu.CompilerParams(dimension_semantics=sem, vmem_limit_bytes=VMEM_LIMIT)


def _dot(a, b):
    return jnp.dot(a, b, preferred_element_type=F32)


def _dot_nt(a, b):
    return lax.dot_general(a, b, (((1,), (1,)), ((), ())), preferred_element_type=F32)


def _dot_exact_lhs(tri, x):
    hi = x.astype(BF16)
    r1 = x - hi.astype(F32)
    mid = r1.astype(BF16)
    lo = (r1 - mid.astype(F32)).astype(BF16)
    return _dot(tri, hi) + _dot(tri, mid) + _dot(tri, lo)


def _lower_tri(n):
    r = lax.broadcasted_iota(I32, (n, n), 0)
    c = lax.broadcasted_iota(I32, (n, n), 1)
    return r >= c


def _sigmoid(x):
    return 1.0 / (1.0 + jnp.exp(-x))


def _softplus(x):
    return jnp.maximum(x, 0.0) + jnp.log(1.0 + jnp.exp(-jnp.abs(x)))


def _log_sigmoid(x):
    return -_softplus(-x)


def _mm_kernel(a_ref, b_ref, o_ref):
    o_ref[...] = _dot(a_ref[...], b_ref[...]).astype(o_ref.dtype)


def _matmul(a, b, out_dtype, tm, tn):
    m, k = a.shape
    n = b.shape[1]
    return pl.pallas_call(
        _mm_kernel,
        grid=(n // tn, m // tm),
        in_specs=[pl.BlockSpec((tm, k), lambda j, i: (i, 0)),
                  pl.BlockSpec((k, tn), lambda j, i: (0, j))],
        out_specs=pl.BlockSpec((tm, tn), lambda j, i: (i, j)),
        out_shape=jax.ShapeDtypeStruct((m, n), out_dtype),
        compiler_params=_cparams("parallel", "parallel"),
        name="in_proj",
    )(a, b)


def _gmlp_kernel(h_ref, lng_ref, lnb_ref, ws_ref, bs_ref, o_ref):
    t = h_ref.shape[0]
    h = h_ref[...].astype(F32)
    h = 0.5 * h * (1.0 + lax.erf(h * (1.0 / math.sqrt(2.0))))
    r = lax.broadcasted_iota(I32, (GM_BLOCK, GM_BLOCK), 0)
    c = lax.broadcasted_iota(I32, (GM_BLOCK, GM_BLOCK), 1)
    allowed = (c // CHUNK) <= (r // CHUNK)
    for g in range(GM_HEADS):
        u = h[:, g * LANES:(g + 1) * LANES]
        v = h[:, GROUP_WIDTH + g * LANES:GROUP_WIDTH + (g + 1) * LANES]
        mu = jnp.mean(v, axis=-1, keepdims=True)
        vc = v - mu
        var = jnp.mean(vc * vc, axis=-1, keepdims=True)
        vn = vc * lax.rsqrt(var + 1e-5) * lng_ref[:, g * LANES:(g + 1) * LANES] \
            + lnb_ref[:, g * LANES:(g + 1) * LANES]
        w = jnp.where(allowed, ws_ref[g], 0.0).astype(BF16)
        for wdw in range(t // GM_BLOCK):
            rows = slice(wdw * GM_BLOCK, (wdw + 1) * GM_BLOCK)
            vmix = _dot(w, vn[rows].astype(BF16)) + bs_ref[g]
            o_ref[rows, g * LANES:(g + 1) * LANES] = (u[rows] * vmix).astype(o_ref.dtype)


def _gmlp(h, lng, lnb, ws, bs, tile=256):
    n = h.shape[0]
    return pl.pallas_call(
        _gmlp_kernel,
        grid=(n // tile,),
        in_specs=[pl.BlockSpec((tile, 2 * GROUP_WIDTH), lambda i: (i, H_GM // (2 * GROUP_WIDTH))),
                  pl.BlockSpec((1, GROUP_WIDTH), lambda i: (0, 0)),
                  pl.BlockSpec((1, GROUP_WIDTH), lambda i: (0, 0)),
                  pl.BlockSpec((GM_HEADS, GM_BLOCK, GM_BLOCK), lambda i: (0, 0, 0)),
                  pl.BlockSpec((GM_HEADS, GM_BLOCK, 1), lambda i: (0, 0, 0))],
        out_specs=pl.BlockSpec((tile, GROUP_WIDTH), lambda i: (i, 0)),
        out_shape=jax.ShapeDtypeStruct((n, GROUP_WIDTH), BF16),
        compiler_params=_cparams("parallel"),
        name="gmlp",
    )(h, lng, lnb, ws, bs)


def _ssd_kernel(xbc_ref, z_ref, dt_ref, cw_ref, cb_ref, dtb_ref, a_ref, dsk_ref, ng_ref, o_ref,
                buf_ref, act_ref, st_ref):
    t = xbc_ref.shape[0]
    lc = SEQ_CHUNK

    @pl.when(pl.program_id(1) == 0)
    def _():
        buf_ref[0:8, :] = jnp.zeros((8, buf_ref.shape[1]), F32)
        st_ref[...] = jnp.zeros(st_ref.shape, F32)

    buf_ref[8:8 + t, :] = xbc_ref[...].astype(F32)
    conv = cb_ref[...] + cw_ref[0:1, :] * buf_ref[5:5 + t, :]
    for k in range(1, SSM_CONV):
        conv = conv + cw_ref[k:k + 1, :] * buf_ref[5 + k:5 + k + t, :]
    act_ref[...] = conv * _sigmoid(conv)
    buf_ref[0:8, :] = buf_ref[t:t + 8, :]

    tri = _lower_tri(lc)
    tri_b = tri.astype(BF16)
    lane = lax.broadcasted_iota(I32, (lc, LANES), 1)
    first_half = lane < SSM_HEADDIM
    a_row = a_ref[...]

    def chunk(ci, carry):
        r0 = pl.multiple_of(ci * lc, lc)
        xa = act_ref[pl.ds(r0, lc), :]
        dt = _softplus(dt_ref[pl.ds(r0, lc), :] + dtb_ref[...])
        acs = _dot_exact_lhs(tri_b, dt * a_row)
        acs_t = acs.T
        ys = []
        for p in range(SSM_HEADS // 2):
            g = p // 2
            h0, h1 = 2 * p, 2 * p + 1
            bm = xa[:, GROUP_WIDTH + g * SSM_STATE:GROUP_WIDTH + (g + 1) * SSM_STATE]
            cm = xa[:, GROUP_WIDTH + 2 * SSM_STATE + g * SSM_STATE:GROUP_WIDTH + 2 * SSM_STATE + (g + 1) * SSM_STATE]
            cm_b = cm.astype(BF16)
            cb = _dot_nt(cm_b, bm.astype(BF16))
            x2 = xa[:, p * LANES:(p + 1) * LANES]
            dt2 = jnp.where(first_half, dt[:, h0:h0 + 1], dt[:, h1:h1 + 1])
            xdt = x2 * dt2
            xdt_b = xdt.astype(BF16)
            yd = []
            for hh in (h0, h1):
                decay = jnp.where(tri, jnp.exp(acs[:, hh:hh + 1] - acs_t[hh:hh + 1, :]), 0.0)
                yd.append(_dot((cb * decay).astype(BF16), xdt_b))
            y_diag = jnp.where(first_half, yd[0], yd[1])
            acs2 = jnp.where(first_half, acs[:, h0:h0 + 1], acs[:, h1:h1 + 1])
            aend2 = jnp.where(first_half[0:1], acs[lc - 1:lc, h0:h0 + 1], acs[lc - 1:lc, h1:h1 + 1])
            st = st_ref[p]
            y_off = _dot(cm_b, st.astype(BF16)) * jnp.exp(acs2)
            ys.append(y_diag + y_off + x2 * dsk_ref[:, p * LANES:(p + 1) * LANES])
            upd = _dot(bm.T.astype(BF16), (xdt * jnp.exp(aend2 - acs2)).astype(BF16))
            st_ref[p] = jnp.exp(aend2) * st + upd
        y = jnp.concatenate(ys, axis=-1)
        zz = z_ref[pl.ds(r0, lc), :].astype(F32)
        y = y * (zz * _sigmoid(zz))
        y = y * lax.rsqrt(jnp.mean(y * y, axis=-1, keepdims=True) + 1e-6) * ng_ref[...]
        o_ref[pl.ds(r0, lc), :] = y.astype(o_ref.dtype)
        return carry

    lax.fori_loop(0, t // lc, chunk, 0)


def _ssd(h, hs, bsz, seq, cw, cb, dtb, a_row, dsk, ng, tile=SEQ_TILE):
    n = h.shape[0]
    nt = seq // tile
    xw = 2 * GROUP_WIDTH
    return pl.pallas_call(
        _ssd_kernel,
        grid=(bsz, nt),
        in_specs=[pl.BlockSpec((tile, xw), lambda b, j: (b * nt + j, H_XBC // xw)),
                  pl.BlockSpec((tile, GROUP_WIDTH), lambda b, j: (b * nt + j, H_Z // GROUP_WIDTH)),
                  pl.BlockSpec((tile, LANES), lambda b, j: (b * nt + j, 1)),
                  pl.BlockSpec((SSM_CONV, xw), lambda b, j: (0, 0)),
                  pl.BlockSpec((1, xw), lambda b, j: (0, 0)),
                  pl.BlockSpec((1, LANES), lambda b, j: (0, 0)),
                  pl.BlockSpec((1, LANES), lambda b, j: (0, 0)),
                  pl.BlockSpec((1, GROUP_WIDTH), lambda b, j: (0, 0)),
                  pl.BlockSpec((1, GROUP_WIDTH), lambda b, j: (0, 0))],
        out_specs=pl.BlockSpec((tile, GROUP_WIDTH), lambda b, j: (b * nt + j, 0)),
        out_shape=jax.ShapeDtypeStruct((n, GROUP_WIDTH), BF16),
        scratch_shapes=[pltpu.VMEM((tile + 8, xw), F32),
                        pltpu.VMEM((tile, xw), F32),
                        pltpu.VMEM((SSM_HEADS // 2, SSM_STATE, LANES), F32)],
        compiler_params=_cparams("arbitrary", "arbitrary"),
        name="ssd",
    )(h, h, hs, cw, cb, dtb, a_row, dsk, ng)


def _mlstm_kernel(q_ref, k_ref, v_ref, og_ref, gt_ref, bias_ref, ng_ref, o_ref, c_ref, n_ref, m_ref):
    t = q_ref.shape[0]
    lc = SEQ_CHUNK
    scale = ML_DIM ** -0.5

    @pl.when(pl.program_id(1) == 0)
    def _():
        c_ref[...] = jnp.zeros(c_ref.shape, F32)
        n_ref[...] = jnp.zeros(n_ref.shape, F32)
        m_ref[...] = jnp.zeros(m_ref.shape, F32)

    tri = _lower_tri(lc)
    tri_b = tri.astype(BF16)

    def chunk(ci, carry):
        r0 = pl.multiple_of(ci * lc, lc)
        gts = gt_ref[pl.ds(r0, lc), :] + bias_ref[...]
        bcum = _dot_exact_lhs(tri_b, _log_sigmoid(gts))
        gts_t = gts.T
        bcum_t = bcum.T
        for hd in range(ML_HEADS):
            cols = slice(hd * ML_DIM, (hd + 1) * ML_DIM)
            qh = q_ref[pl.ds(r0, lc), cols]
            kh = (k_ref[pl.ds(r0, lc), cols].astype(F32) * scale).astype(BF16)
            vh = v_ref[pl.ds(r0, lc), cols]
            i_col = gts[:, hd:hd + 1]
            i_row = gts_t[hd:hd + 1, :]
            b_col = bcum[:, ML_HEADS + hd:ML_HEADS + hd + 1]
            b_row = bcum_t[ML_HEADS + hd:ML_HEADS + hd + 1, :]
            b_end = b_col[lc - 1:lc, :]
            m_prev = m_ref[hd][0:1, 0:1]
            c_prev = c_ref[hd]
            n_prev = n_ref[hd][0:1, :]
            log_d = jnp.where(tri, b_col - b_row + i_row, -jnp.inf)
            inter = b_col + m_prev
            m_j = jnp.maximum(jnp.max(log_d, axis=-1, keepdims=True), inter)
            s_mat = jnp.exp(log_d - m_j) * _dot_nt(qh, kh)
            gsc = jnp.exp(inter - m_j)
            num = _dot(s_mat.astype(BF16), vh) + gsc * _dot_nt(qh, c_prev.astype(BF16))
            den = jnp.sum(s_mat, axis=-1, keepdims=True) \
                + gsc * jnp.sum(qh.astype(F32) * n_prev, axis=-1, keepdims=True)
            hh = num / jnp.maximum(jnp.abs(den), jnp.exp(-m_j))
            mu = jnp.mean(hh, axis=-1, keepdims=True)
            hc = hh - mu
            var = jnp.mean(hc * hc, axis=-1, keepdims=True)
            hn = hc * lax.rsqrt(var + 1e-5) * ng_ref[:, cols]
            og = og_ref[pl.ds(r0, lc), cols].astype(F32)
            o_ref[pl.ds(r0, lc), cols] = (hn * _sigmoid(og)).astype(o_ref.dtype)
            a_col = b_end - b_col + i_col
            a_row = b_end - b_row + i_row
            m_new = jnp.maximum(b_end + m_prev, jnp.max(a_row, axis=-1, keepdims=True))
            decay = jnp.exp(b_end + m_prev - m_new)
            w_col = jnp.exp(a_col - m_new)
            kf = kh.astype(F32)
            vw_t = (vh.astype(F32) * w_col).T.astype(BF16)
            c_ref[hd] = decay * c_prev + _dot(vw_t, kh)
            n_new = decay * n_prev + jnp.sum(w_col * kf, axis=0, keepdims=True)
            n_ref[hd] = jnp.broadcast_to(n_new, (8, ML_DIM))
            m_ref[hd] = jnp.broadcast_to(m_new, (8, LANES))
        return carry

    lax.fori_loop(0, t // lc, chunk, 0)


def _mlstm(h, hs, bsz, seq, bias, ng, tile=SEQ_TILE):
    n = h.shape[0]
    nt = seq // tile
    gw = GROUP_WIDTH

    def col(c0):
        return pl.BlockSpec((tile, gw), lambda b, j: (b * nt + j, c0 // gw))

    return pl.pallas_call(
        _mlstm_kernel,
        grid=(bsz, nt),
        in_specs=[col(H_Q), col(H_K), col(H_V), col(H_O),
                  pl.BlockSpec((tile, LANES), lambda b, j: (b * nt + j, 2)),
                  pl.BlockSpec((1, LANES), lambda b, j: (0, 0)),
                  pl.BlockSpec((1, gw), lambda b, j: (0, 0))],
        out_specs=pl.BlockSpec((tile, gw), lambda b, j: (b * nt + j, 0)),
        out_shape=jax.ShapeDtypeStruct((n, gw), BF16),
        scratch_shapes=[pltpu.VMEM((ML_HEADS, ML_DIM, ML_DIM), F32),
                        pltpu.VMEM((ML_HEADS, 8, ML_DIM), F32),
                        pltpu.VMEM((ML_HEADS, 8, LANES), F32)],
        compiler_params=_cparams("arbitrary", "arbitrary"),
        name="mlstm",
    )(h, h, h, h, hs, bias, ng)


def _att_prep_kernel(cq_ref, ak_ref, hs_ref, pos_ref, qg_ref, wuq_ref, kg_ref, kb_ref,
                     fm_ref, sm_ref, fi_ref, si_ref, fk_ref, sk_ref,
                     q_ref, k_ref, qi_ref, ki_ref, w_ref):
    t = cq_ref.shape[0]
    lane = lax.broadcasted_iota(I32, (t, LANES), 1)
    pos = pos_ref[...].astype(F32)

    cq = cq_ref[:, 0:Q_RANK].astype(F32)
    cq = cq * lax.rsqrt(jnp.mean(cq * cq, axis=-1, keepdims=True) + 1e-6) * qg_ref[...]
    q_all = _dot(cq.astype(BF16), wuq_ref[...])

    def rope(x, cos, sin_signed, half, period):
        first = (lane % period) < half
        partner = jnp.where(first, pltpu.roll(x, LANES - half, 1), pltpu.roll(x, half, 1))
        return x * cos + partner * sin_signed

    ang = pos * fm_ref[...]
    cos_m, sin_m = jnp.cos(ang), jnp.sin(ang) * sm_ref[...]
    half_m = ATT_DIM // 8
    for hd in range(ATT_HEADS):
        cols = slice(hd * ATT_DIM, (hd + 1) * ATT_DIM)
        q_ref[:, cols] = (rope(q_all[:, cols], cos_m, sin_m, half_m, LANES) * (ATT_DIM ** -0.5)).astype(q_ref.dtype)
        k_ref[:, cols] = rope(ak_ref[:, cols].astype(F32), cos_m, sin_m, half_m, LANES).astype(k_ref.dtype)

    ang = pos * fi_ref[...]
    cos_i, sin_i = jnp.cos(ang), jnp.sin(ang) * si_ref[...]
    half_i = IDX_DIM // 8
    low = lane < IDX_DIM
    for pr in range(IDX_HEADS // 2):
        x = rope(q_all[:, GROUP_WIDTH + pr * LANES:GROUP_WIDTH + (pr + 1) * LANES], cos_i, sin_i, half_i, IDX_DIM)
        even = jnp.where(low, x, 0.0).astype(qi_ref.dtype)
        odd = jnp.where(low, pltpu.roll(x, IDX_DIM, 1), 0.0).astype(qi_ref.dtype)
        for blk in range(t // DSA_SB):
            rows = slice(blk * DSA_SB, (blk + 1) * DSA_SB)
            qi_ref[blk, 2 * pr] = even[rows]
            qi_ref[blk, 2 * pr + 1] = odd[rows]

    hs = hs_ref[...]
    kx = jnp.where(low, hs, 0.0)
    mu = jnp.sum(kx, axis=-1, keepdims=True) * (1.0 / IDX_DIM)
    kc = jnp.where(low, hs - mu, 0.0)
    var = jnp.sum(kc * kc, axis=-1, keepdims=True) * (1.0 / IDX_DIM)
    kn = kc * lax.rsqrt(var + 1e-5) * kg_ref[...] + kb_ref[...]
    ang = pos * fk_ref[...]
    ki_ref[...] = rope(kn, jnp.cos(ang), jnp.sin(ang) * sk_ref[...], half_i, LANES).astype(ki_ref.dtype)
    wi = jnp.where(lane < IDX_HEADS, pltpu.roll(hs, LANES - IDX_DIM, 1), 0.0) * (IDX_HEADS ** -0.5 * IDX_DIM ** -0.5)
    for blk in range(t // DSA_SB):
        w_ref[blk] = wi[blk * DSA_SB:(blk + 1) * DSA_SB, :].T[0:IDX_HEADS, :]


def _att_prep(h, hs, pos, qg, wuq, kg, kb, rope_rows, tile=256):
    n = h.shape[0]
    gw = GROUP_WIDTH
    row = lambda w: pl.BlockSpec((1, w), lambda i: (0, 0))
    return pl.pallas_call(
        _att_prep_kernel,
        grid=(n // tile,),
        in_specs=[pl.BlockSpec((tile, gw), lambda i: (i, H_CQ // gw)),
                  pl.BlockSpec((tile, gw), lambda i: (i, H_AK // gw)),
                  pl.BlockSpec((tile, LANES), lambda i: (i, 0)),
                  pl.BlockSpec((tile, 1), lambda i: (i, 0)),
                  row(Q_RANK),
                  pl.BlockSpec(wuq.shape, lambda i: (0, 0)),
                  row(LANES), row(LANES)] + [row(LANES)] * 6,
        out_specs=[pl.BlockSpec((tile, gw), lambda i: (i, 0)),
                   pl.BlockSpec((tile, gw), lambda i: (i, 0)),
                   pl.BlockSpec((tile // DSA_SB, IDX_HEADS, DSA_SB, LANES), lambda i: (i, 0, 0, 0)),
                   pl.BlockSpec((tile, LANES), lambda i: (i, 0)),
                   pl.BlockSpec((tile // DSA_SB, IDX_HEADS, DSA_SB), lambda i: (i, 0, 0))],
        out_shape=[jax.ShapeDtypeStruct((n, gw), BF16),
                   jax.ShapeDtypeStruct((n, gw), BF16),
                   jax.ShapeDtypeStruct((n // DSA_SB, IDX_HEADS, DSA_SB, LANES), BF16),
                   jax.ShapeDtypeStruct((n, LANES), BF16),
                   jax.ShapeDtypeStruct((n // DSA_SB, IDX_HEADS, DSA_SB), F32)],
        compiler_params=_cparams("parallel"),
        name="att_prep",
    )(h, h, hs, pos, qg, wuq, kg, kb, *rope_rows)


def _dsa_select_kernel(qi_ref, w_ref, ki_ref, bias_ref, key_ref, hi_ref, lo_ref, tri_ref, *, top_k):
    sb = w_ref.shape[2]
    tk = key_ref.shape[1]
    nt_all = bias_ref.shape[1]
    i = pl.program_id(1)
    n_tiles = (i * sb + sb + tk - 1) // tk
    row = lax.broadcasted_iota(I32, (1, sb), 1)
    lim = ((i * sb + row) // CHUNK + 1) * CHUNK
    sub_t = lax.broadcasted_iota(I32, (tk, sb), 0)
    w = w_ref[0]
    half_min = -(2 ** 15)

    def fill(kt, carry):
        k0 = pl.multiple_of(kt * tk, tk)
        kk = ki_ref[pl.ds(k0, tk), :]
        acc = None
        for pr in range(IDX_HEADS // 2):
            rel = _dot_nt(kk, qi_ref[2 * pr * sb:(2 * pr + 2) * sb, :])
            for u in range(2):
                term = jnp.maximum(rel[:, u * sb:(u + 1) * sb], 0.0) * w[2 * pr + u:2 * pr + u + 1, :]
                acc = term if acc is None else acc + term
        score = jnp.where(sub_t + k0 < lim, acc + 0.0, -jnp.inf)
        bits = pltpu.bitcast(score, I32)
        keys = bits ^ ((bits >> 31) & INT_MAX)
        key_ref[kt] = keys
        hi_ref[kt] = (keys >> 16).astype(jnp.int16)
        return carry

    lax.fori_loop(0, n_tiles, fill, 0)

    group = DSA_TILE_GROUP
    n_groups = (n_tiles + group - 1) // group

    def pad_tile(kt, carry):
        key_ref[kt] = jnp.full((tk, sb), NEG_INF_KEY, I32)
        hi_ref[kt] = jnp.full((tk, sb), NEG_INF_KEY >> 16, jnp.int16)
        return carry

    lax.fori_loop(n_tiles, n_groups * group, pad_tile, 0)

    def tree_sum(parts):
        while len(parts) > 1:
            parts = [parts[j] + parts[j + 1] for j in range(0, len(parts), 2)]
        return parts[0]

    def count(ref, pred, dtype):
        gran = 32 // jnp.dtype(dtype).itemsize

        def body(g, acc):
            for u in range(group):
                kt = g * group + u
                m = pred(ref[kt], kt).astype(dtype)
                acc = acc + tree_sum([m[c * gran:(c + 1) * gran] for c in range(tk // gran)])
            return acc
        acc = lax.fori_loop(0, n_groups, body, jnp.zeros((gran, sb), dtype))
        return jnp.sum(acc.astype(I32), axis=0, keepdims=True)

    def count16_ge(ref, cand):
        c16 = cand.astype(jnp.int16)
        return count(ref, lambda v, kt: v >= c16, jnp.int16)

    def greedy16_until_resolved(ref, need):
        def cond(c):
            return (c[0] < 16) & (c[3] == 0)

        def body(c):
            s, t, done, _ = c
            cand = t + lax.shift_left(jnp.int32(1), 15 - s)
            cnt = count16_ge(ref, cand)
            take = (cnt >= need) & (done == 0)
            done = jnp.where(take & (cnt == need), 1, done)
            return s + 1, jnp.where(take, cand, t), done, jnp.min(done)

        init = (jnp.int32(0), jnp.full((1, sb), half_min, I32), jnp.zeros((1, sb), I32), jnp.int32(0))
        return lax.while_loop(cond, body, init)[1]

    def greedy16(ref, need):
        def step(s, t):
            cand = t + lax.shift_left(jnp.int32(1), 15 - s)
            return jnp.where(count16_ge(ref, cand) >= need, cand, t)
        return lax.fori_loop(0, 16, step, jnp.full((1, sb), half_min, I32))

    t_hi = greedy16(hi_ref, top_k)
    t_hi16 = t_hi.astype(jnp.int16)
    above = count(hi_ref, lambda v, kt: v > t_hi16, jnp.int16)
    need_lo = top_k - above

    def fill_lo(kt, carry):
        lo = ((key_ref[kt] & 0xFFFF) + half_min).astype(jnp.int16)
        lo_ref[kt] = jnp.where(hi_ref[kt] == t_hi16, lo, jnp.int16(half_min))
        return carry

    lax.fori_loop(0, n_groups * group, fill_lo, 0)
    t_lo = greedy16_until_resolved(lo_ref, need_lo)
    thr = (t_hi << 16) | ((t_lo - half_min) & 0xFFFF)
    t_lo16 = t_lo.astype(jnp.int16)
    need_tied = (need_lo - count(lo_ref, lambda v, kt: v > t_lo16, jnp.int16)).astype(F32)

    r = lax.broadcasted_iota(I32, (tk, tk), 0)
    c = lax.broadcasted_iota(I32, (tk, tk), 1)
    tri_ref[...] = (r >= c).astype(tri_ref.dtype)

    def emit(kt, seen):
        keys = key_ref[kt]
        tied = keys == thr
        seen = seen + _dot(tri_ref[...], tied.astype(tri_ref.dtype))
        sel = ((keys > thr) | (tied & (seen <= need_tied))) & (sub_t + kt * tk < lim)
        bias_t = jnp.where(sel, 0.0, -jnp.inf)
        for cb in range(tk // LANES):
            bias_ref[0, kt, :, cb * LANES:(cb + 1) * LANES] = \
                bias_t[cb * LANES:(cb + 1) * LANES, :].T.astype(bias_ref.dtype)
        return seen[tk - 1:tk, :]

    lax.fori_loop(0, n_tiles, emit, jnp.zeros((1, sb), F32))

    def blank(kt, carry):
        bias_ref[0, kt] = jnp.full(bias_ref.shape[2:], -jnp.inf, bias_ref.dtype)
        return carry

    lax.fori_loop(n_tiles, nt_all, blank, 0)


def _dsa_select(qi, w, ki, bsz, seq, top_k):
    nb, _, sb = w.shape
    tk = DSA_TK
    nq, nt = seq // sb, seq // tk
    nt_pad = -(-nt // DSA_TILE_GROUP) * DSA_TILE_GROUP
    kern = functools.partial(_dsa_select_kernel, top_k=top_k)
    return pl.pallas_call(
        kern,
        grid=(bsz, nq),
        in_specs=[pl.BlockSpec((IDX_HEADS * sb, LANES), lambda b, i: (b * nq + i, 0)),
                  pl.BlockSpec((1, IDX_HEADS, sb), lambda b, i: (b * nq + i, 0, 0)),
                  pl.BlockSpec((seq, LANES), lambda b, i: (b, 0))],
        out_specs=pl.BlockSpec((1, nt, sb, tk), lambda b, i: (b * nq + i, 0, 0, 0)),
        out_shape=jax.ShapeDtypeStruct((nb, nt, sb, tk), BF16),
        scratch_shapes=[pltpu.VMEM((nt_pad, tk, sb), I32),
                        pltpu.VMEM((nt_pad, tk, sb), jnp.int16),
                        pltpu.VMEM((nt_pad, tk, sb), jnp.int16),
                        pltpu.VMEM((tk, tk), BF16)],
        compiler_params=_cparams("parallel", "parallel"),
        name="dsa_select",
    )(qi.reshape(nb * IDX_HEADS * sb, LANES), w, ki)


def _dsa_attn_kernel(qt_ref, kt_ref, first_ref, last_ref, q_ref, bias_ref, k_ref, v_ref, o_ref,
                     m_ref, l_ref, acc_ref):
    s = pl.program_id(1)
    qb, tk = q_ref.shape[0], k_ref.shape[0]

    @pl.when(first_ref[s] == 1)
    def _():
        m_ref[...] = jnp.full(m_ref.shape, NEG_BIG, F32)
        l_ref[...] = jnp.zeros(l_ref.shape, F32)
        acc_ref[...] = jnp.zeros(acc_ref.shape, F32)

    bias = bias_ref[...].reshape(qb, tk).astype(F32)

    for hd in range(ATT_HEADS):
        cols = slice(hd * ATT_DIM, (hd + 1) * ATT_DIM)
        logits = _dot_nt(q_ref[:, cols], k_ref[:, cols]) + bias
        m_prev = m_ref[:, cols]
        m_new = jnp.maximum(m_prev, jnp.max(logits, axis=-1, keepdims=True))
        p = jnp.exp(logits - m_new[:, 0:1])
        alpha = jnp.exp(m_prev - m_new)
        l_ref[:, cols] = alpha * l_ref[:, cols] + jnp.sum(p, axis=-1, keepdims=True)
        acc_ref[:, cols] = alpha * acc_ref[:, cols] + _dot(p.astype(BF16), v_ref[:, cols])
        m_ref[:, cols] = m_new

    @pl.when(last_ref[s] == 1)
    def _():
        o_ref[...] = (acc_ref[...] / l_ref[...]).astype(o_ref.dtype)


def _dsa_attention(q, bias, k, h, bsz, seq):
    n = q.shape[0]
    qb, tk = DSA_QB, DSA_TK
    nq, nkt = seq // qb, seq // tk
    sub = qb // DSA_SB
    pairs = [(i, t) for i in range(nq) for t in range((i * qb + qb - 1) // tk + 1)]
    qt = jnp.asarray([p[0] for p in pairs], I32)
    ktab = jnp.asarray([p[1] for p in pairs], I32)
    first = jnp.asarray([1 if p[1] == 0 else 0 for p in pairs], I32)
    last = jnp.asarray([1 if p[1] == (p[0] * qb + qb - 1) // tk else 0 for p in pairs], I32)
    gw = GROUP_WIDTH

    def qmap(w_):
        return pl.BlockSpec((qb, w_), lambda b, s, qt, kt, f, l: (b * nq + qt[s], 0))

    def kmap(w_, cb):
        return pl.BlockSpec((tk, w_), lambda b, s, qt, kt, f, l: (b * nkt + kt[s], cb))

    grid_spec = pltpu.PrefetchScalarGridSpec(
        num_scalar_prefetch=4,
        grid=(bsz, len(pairs)),
        in_specs=[qmap(gw),
                  pl.BlockSpec((sub, 1, DSA_SB, tk), lambda b, s, qt, kt, f, l: (b * nq + qt[s], kt[s], 0, 0)),
                  kmap(gw, 0), kmap(gw, H_AV // gw)],
        out_specs=qmap(gw),
        scratch_shapes=[pltpu.VMEM((qb, gw), F32), pltpu.VMEM((qb, gw), F32), pltpu.VMEM((qb, gw), F32)],
    )
    return pl.pallas_call(
        _dsa_attn_kernel,
        grid_spec=grid_spec,
        out_shape=jax.ShapeDtypeStruct((n, gw), BF16),
        compiler_params=_cparams("arbitrary", "arbitrary"),
        name="dsa_attention",
    )(qt, ktab, first, last, q, bias, k, h)


def _dsa(h, hs, pos, bsz, seq, q_norm_g, w_uq, idxk_g, idxk_b):
    q, k, qi, ki, wi = _att_prep(h, hs, pos, q_norm_g.reshape(1, -1), w_uq.astype(BF16),
                                 _pad_row(idxk_g, LANES), _pad_row(idxk_b, LANES), _rope_rows())
    bias = _dsa_select(qi, wi, ki, bsz, seq, min(IDX_TOPK_MAX, seq // 4))
    return _dsa_attention(q, bias, k, h, bsz, seq)


def _mix_kernel(ya_ref, yb_ref, yc_ref, yd_ref, x_ref, wo_ref, g_ref, b_ref, rw_ref, rb_ref,
                x1_ref, ti_ref, gt_ref, *, alpha):
    gw = GROUP_WIDTH
    mix = _dot(ya_ref[...], wo_ref[0:gw, :])
    mix = mix + _dot(yb_ref[...], wo_ref[gw:2 * gw, :])
    mix = mix + _dot(yc_ref[...], wo_ref[2 * gw:3 * gw, :])
    mix = mix + _dot(yd_ref[...], wo_ref[3 * gw:4 * gw, :])
    y = alpha * x_ref[...] + mix
    mu = jnp.mean(y, axis=-1, keepdims=True)
    yc = y - mu
    var = jnp.mean(yc * yc, axis=-1, keepdims=True)
    x1 = yc * lax.rsqrt(var + 1e-5) * g_ref[...] + b_ref[...]
    x1_ref[...] = x1

    logits = jnp.dot(x1, rw_ref[...], precision=lax.Precision.HIGHEST, preferred_element_type=F32) + rb_ref[...]
    lane = lax.broadcasted_iota(I32, logits.shape, 1)
    vals, idxs = [], []
    for _ in range(TOP_K):
        mx = jnp.max(logits, axis=-1, keepdims=True)
        ix = jnp.min(jnp.where(logits == mx, lane, LANES), axis=-1, keepdims=True)
        vals.append(mx)
        idxs.append(ix)
        logits = jnp.where(lane == ix, -jnp.inf, logits)
    es = [jnp.exp(v - vals[0]) for v in vals]
    tot = es[0] + es[1] + es[2] + es[3]
    ti = jnp.zeros(logits.shape, I32)
    gt = jnp.zeros(logits.shape, F32)
    for k in range(TOP_K):
        ti = jnp.where(lane == k, idxs[k], ti)
        gt = jnp.where(lane == k, es[k] / tot, gt)
    ti_ref[...] = ti
    gt_ref[...] = gt


def _mix_ln_router(ys, x, wo, g, b, rw, rb, alpha, tile=256):
    n, d = x.shape
    gw = GROUP_WIDTH
    yspec = pl.BlockSpec((tile, gw), lambda i: (i, 0))
    row = lambda w: pl.BlockSpec((1, w), lambda i: (0, 0))
    return pl.pallas_call(
        functools.partial(_mix_kernel, alpha=alpha),
        grid=(n // tile,),
        in_specs=[yspec, yspec, yspec, yspec,
                  pl.BlockSpec((tile, d), lambda i: (i, 0)),
                  pl.BlockSpec(wo.shape, lambda i: (0, 0)),
                  row(d), row(d),
                  pl.BlockSpec(rw.shape, lambda i: (0, 0)),
                  row(LANES)],
        out_specs=[pl.BlockSpec((tile, d), lambda i: (i, 0)),
                   pl.BlockSpec((tile, LANES), lambda i: (i, 0)),
                   pl.BlockSpec((tile, LANES), lambda i: (i, 0))],
        out_shape=[jax.ShapeDtypeStruct((n, d), F32),
                   jax.ShapeDtypeStruct((n, LANES), I32),
                   jax.ShapeDtypeStruct((n, LANES), F32)],
        compiler_params=_cparams("parallel"),
        name="mix_ln_router",
    )(*ys, x, wo, g, b, rw, rb)


def _row_copy(src, s, dst, d, sem):
    return pltpu.make_async_copy(src.at[pl.ds(s, 1), :], dst.at[pl.ds(d, 1), :], sem)


def _dispatch_kernel(pos_ref, x_ref, xs_in_ref, xs_ref, sem):
    del xs_in_ref
    tc = x_ref.shape[0]

    def issue(r, carry):
        for k in range(TOP_K):
            _row_copy(x_ref, r, xs_ref, pos_ref[0, 0, r * TOP_K + k], sem).start()
        return carry

    lax.fori_loop(0, tc, issue, 0)

    def drain(r, carry):
        for k in range(TOP_K):
            _row_copy(x_ref, 0, xs_ref, 0, sem).wait()
        return carry

    lax.fori_loop(0, tc, drain, 0)


def _dispatch(pos, x1, n_slots, tile=256):
    n, d = x1.shape
    pos3 = pos.reshape(n // tile, 1, tile * TOP_K)
    xs0 = jnp.zeros((n_slots, d), F32)
    return pl.pallas_call(
        _dispatch_kernel,
        grid=(n // tile,),
        in_specs=[pl.BlockSpec((1, 1, tile * TOP_K), lambda i: (i, 0, 0), memory_space=pltpu.SMEM),
                  pl.BlockSpec((tile, d), lambda i: (i, 0)),
                  pl.BlockSpec(memory_space=pl.ANY)],
        out_specs=pl.BlockSpec(memory_space=pl.ANY),
        out_shape=jax.ShapeDtypeStruct((n_slots, d), F32),
        scratch_shapes=[pltpu.SemaphoreType.DMA(())],
        input_output_aliases={2: 0},
        compiler_params=_cparams("arbitrary"),
        name="moe_dispatch",
    )(pos3, x1, xs0)


def _new_expert(te_ref, i):
    return (i == 0) | (te_ref[i] != te_ref[jnp.maximum(i - 1, 0)])


def _expert_up_kernel(te_ref, nu_ref, x_ref, wg_ref, wu_ref, bg_ref, bu_ref, o_ref, wgb_ref, wub_ref):
    i = pl.program_id(1)

    @pl.when(_new_expert(te_ref, i))
    def _():
        wgb_ref[...] = wg_ref[...].astype(BF16)
        wub_ref[...] = wu_ref[...].astype(BF16)

    @pl.when(i < nu_ref[0])
    def _():
        xb = x_ref[...].astype(BF16)
        g = _dot(xb, wgb_ref[...]) + bg_ref[...]
        u = _dot(xb, wub_ref[...]) + bu_ref[...]
        g = jnp.minimum(g, SWIGLU_LIMIT)
        u = jnp.clip(u, -SWIGLU_LIMIT, SWIGLU_LIMIT)
        o_ref[...] = ((u + 1.0) * g * _sigmoid(SWIGLU_ALPHA * g)).astype(o_ref.dtype)


def _expert_up(te, nu, xs, w_gu, b_gu, layer, splits=4):
    p, d = xs.shape
    ff = w_gu.shape[3] // 2
    fj = ff // splits
    tm = MOE_TM
    tile_of = lambda i, nu: jnp.minimum(i, nu[0] - 1)
    grid_spec = pltpu.PrefetchScalarGridSpec(
        num_scalar_prefetch=2,
        grid=(splits, p // tm),
        in_specs=[pl.BlockSpec((tm, d), lambda j, i, te, nu: (tile_of(i, nu), 0)),
                  pl.BlockSpec((None, None, d, fj), lambda j, i, te, nu: (layer, te[i], 0, j)),
                  pl.BlockSpec((None, None, d, fj), lambda j, i, te, nu: (layer, te[i], 0, splits + j)),
                  pl.BlockSpec((None, None, 1, fj), lambda j, i, te, nu: (layer, te[i], 0, j)),
                  pl.BlockSpec((None, None, 1, fj), lambda j, i, te, nu: (layer, te[i], 0, splits + j))],
        out_specs=pl.BlockSpec((tm, fj), lambda j, i, te, nu: (tile_of(i, nu), j)),
        scratch_shapes=[pltpu.VMEM((d, fj), BF16), pltpu.VMEM((d, fj), BF16)],
    )
    return pl.pallas_call(
        _expert_up_kernel,
        grid_spec=grid_spec,
        out_shape=jax.ShapeDtypeStruct((p, ff), BF16),
        compiler_params=_cparams("arbitrary", "arbitrary"),
        name="expert_up",
    )(te, nu, xs, w_gu, w_gu, b_gu, b_gu)


def _expert_down_kernel(te_ref, nu_ref, a_ref, wd_ref, bd_ref, o_ref, wdb_ref):
    i = pl.program_id(1)

    @pl.when(_new_expert(te_ref, i))
    def _():
        wdb_ref[...] = wd_ref[...].astype(BF16)

    @pl.when(i < nu_ref[0])
    def _():
        o_ref[...] = _dot(a_ref[...], wdb_ref[...]) + bd_ref[...]


def _expert_down(te, nu, act, w_down, b_down, layer, splits=2):
    p, ff = act.shape
    d = w_down.shape[3]
    dj = d // splits
    tm = MOE_TM
    tile_of = lambda i, nu: jnp.minimum(i, nu[0] - 1)
    grid_spec = pltpu.PrefetchScalarGridSpec(
        num_scalar_prefetch=2,
        grid=(splits, p // tm),
        in_specs=[pl.BlockSpec((tm, ff), lambda j, i, te, nu: (tile_of(i, nu), 0)),
                  pl.BlockSpec((None, None, ff, dj), lambda j, i, te, nu: (layer, te[i], 0, j)),
                  pl.BlockSpec((None, None, 1, dj), lambda j, i, te, nu: (layer, te[i], 0, j))],
        out_specs=pl.BlockSpec((tm, dj), lambda j, i, te, nu: (tile_of(i, nu), j)),
        scratch_shapes=[pltpu.VMEM((ff, dj), BF16)],
    )
    return pl.pallas_call(
        _expert_down_kernel,
        grid_spec=grid_spec,
        out_shape=jax.ShapeDtypeStruct((p, d), F32),
        compiler_params=_cparams("arbitrary", "arbitrary"),
        name="expert_down",
    )(te, nu, act, w_down, b_down)


def _combine_kernel(pos_ref, x_ref, gt_ref, g_ref, b_ref, eo_ref, o_ref, ob_ref, buf_ref, sem, *, alpha):
    tc = x_ref.shape[0]

    def issue(r, carry):
        for k in range(TOP_K):
            _row_copy(eo_ref, pos_ref[0, 0, r * TOP_K + k], buf_ref.at[k], r, sem).start()
        return carry

    lax.fori_loop(0, tc, issue, 0)

    def drain(r, carry):
        for k in range(TOP_K):
            _row_copy(eo_ref, 0, buf_ref.at[k], 0, sem).wait()
        return carry

    lax.fori_loop(0, tc, drain, 0)

    gt = gt_ref[...]
    y = alpha * x_ref[...]
    for k in range(TOP_K):
        y = y + gt[:, k:k + 1] * buf_ref[k]
    mu = jnp.mean(y, axis=-1, keepdims=True)
    yc = y - mu
    var = jnp.mean(yc * yc, axis=-1, keepdims=True)
    out = yc * lax.rsqrt(var + 1e-5) * g_ref[...] + b_ref[...]
    o_ref[...] = out
    ob_ref[...] = out.astype(ob_ref.dtype)


def _combine_ln(pos, x1, gates, g, b, eo, alpha, tile=256):
    n, d = x1.shape
    pos3 = pos.reshape(n // tile, 1, tile * TOP_K)
    row = lambda w: pl.BlockSpec((1, w), lambda i: (0, 0))
    return pl.pallas_call(
        functools.partial(_combine_kernel, alpha=alpha),
        grid=(n // tile,),
        in_specs=[pl.BlockSpec((1, 1, tile * TOP_K), lambda i: (i, 0, 0), memory_space=pltpu.SMEM),
                  pl.BlockSpec((tile, d), lambda i: (i, 0)),
                  pl.BlockSpec((tile, LANES), lambda i: (i, 0)),
                  row(d), row(d),
                  pl.BlockSpec(memory_space=pl.ANY)],
        out_specs=[pl.BlockSpec((tile, d), lambda i: (i, 0)),
                   pl.BlockSpec((tile, d), lambda i: (i, 0))],
        out_shape=[jax.ShapeDtypeStruct((n, d), F32), jax.ShapeDtypeStruct((n, d), BF16)],
        scratch_shapes=[pltpu.VMEM((TOP_K, tile, d), F32), pltpu.SemaphoreType.DMA(())],
        compiler_params=_cparams("arbitrary"),
        name="moe_combine_ln",
    )(pos3, x1, gates, g, b, eo)


def _slot_positions(top_i, n_tiles):
    tm = MOE_TM
    e_flat = top_i.reshape(-1)
    onehot = (e_flat[:, None] == jnp.arange(N_EXPERTS, dtype=I32)[None, :]).astype(I32)
    csum = jnp.cumsum(onehot, axis=0)
    rank = jnp.sum(csum * onehot, axis=1) - 1
    counts = csum[-1]
    padded = ((counts + tm - 1) // tm) * tm
    ends = jnp.cumsum(padded)
    pos = (ends - padded)[e_flat] + rank
    n_used = (ends[-1] // tm).astype(I32)
    tile_e = jnp.sum((ends[None, :] <= (jnp.arange(n_tiles, dtype=I32) * tm)[:, None]).astype(I32), axis=1)
    last_e = jnp.max(jnp.where(counts > 0, jnp.arange(N_EXPERTS, dtype=I32), 0))
    tile_e = jnp.minimum(tile_e, last_e)
    return pos.astype(I32), tile_e, n_used.reshape(1)


def _in_proj_columns():
    gw = GROUP_WIDTH
    gm, ssm = 0, 2 * gw
    xbc_w = gw + 4 * SSM_STATE
    ml = ssm + gw + xbc_w + SSM_HEADS
    att = ml + 4 * gw + 2 * ML_HEADS
    r = lambda a, b: list(range(a, b))
    big = (r(gm, gm + 2 * gw) + r(ssm + gw, ssm + gw + xbc_w) + r(ssm, ssm + gw) + r(ml, ml + 4 * gw)
           + r(att + Q_RANK, att + Q_RANK + 2 * gw) + r(att, att + Q_RANK))
    kidx0 = att + Q_RANK + 2 * gw
    small = {0: r(kidx0, kidx0 + IDX_DIM + IDX_HEADS),
             LANES: r(ssm + gw + xbc_w, ssm + gw + xbc_w + SSM_HEADS),
             2 * LANES: r(ml + 4 * gw, ml + 4 * gw + 2 * ML_HEADS)}
    return np.asarray(big, np.int32), small


def _pad_row(v, width, fill=0.0):
    v = v.reshape(1, -1).astype(F32)
    return jnp.pad(v, ((0, 0), (0, width - v.shape[1])), constant_values=fill)


def _rope_rows():
    def rows(rot, starts):
        half = rot // 2
        inv = ROPE_THETA ** (-jnp.arange(half, dtype=F32) * 2.0 / rot)
        f = jnp.zeros((LANES,), F32)
        s = jnp.zeros((LANES,), F32)
        for st in starts:
            f = f.at[st:st + half].set(inv).at[st + half:st + rot].set(inv)
            s = s.at[st:st + half].set(-1.0).at[st + half:st + rot].set(1.0)
        return f.reshape(1, LANES), s.reshape(1, LANES)
    fm, sm = rows(ATT_DIM // 4, [0])
    fi, si = rows(IDX_DIM // 4, [0, IDX_DIM])
    fk, sk = rows(IDX_DIM // 4, [0])
    return fm, sm, fi, si, fk, sk


def kernel(x, positions, w_in, gm_ln_g, gm_ln_b, gm_ws, gm_bs, ssm_conv_w, ssm_conv_b, ssm_dt_bias, ssm_a_log,
           ssm_d, ssm_norm_g, ml_b_i, ml_b_f, ml_norm_g, att_q_norm_g, att_w_uq, idx_k_ln_g, idx_k_ln_b, w_out,
           ln1_g, ln1_b, ln2_g, ln2_b, router_w, router_b, expert_w_gu, expert_b_gu, expert_w_down,
           expert_b_down):
    bsz, seq, d = x.shape
    depth = w_in.shape[0]
    n = bsz * seq
    alpha = (2.0 * depth) ** 0.25
    n_slots = n * TOP_K + N_EXPERTS * MOE_TM
    n_tiles = n_slots // MOE_TM
    big_cols, small_cols = _in_proj_columns()
    pos = positions.reshape(n, 1).astype(I32)

    xf = x.reshape(n, d)
    xb = xf.astype(BF16)
    for l in range(depth):
        w_big = jnp.pad(jnp.take(w_in[l], big_cols, axis=1), ((0, 0), (0, H_COLS - big_cols.size))).astype(BF16)
        w_small = jnp.zeros((d, HS_COLS), F32)
        for c0, cols in small_cols.items():
            w_small = w_small.at[:, c0:c0 + len(cols)].set(jnp.take(w_in[l], np.asarray(cols, np.int32), axis=1))
        h = _matmul(xb, w_big, BF16, 1024, 1024)
        hs = _matmul(xb, w_small.astype(BF16), F32, 1024, HS_COLS)

        y_gm = _gmlp(h, gm_ln_g[l].reshape(1, -1), gm_ln_b[l].reshape(1, -1), gm_ws[l],
                     gm_bs[l].reshape(GM_HEADS, GM_BLOCK, 1))
        y_ssm = _ssd(h, hs, bsz, seq, ssm_conv_w[l], ssm_conv_b[l].reshape(1, -1),
                     _pad_row(ssm_dt_bias[l], LANES), _pad_row(-jnp.exp(ssm_a_log[l].astype(F32)), LANES),
                     jnp.repeat(ssm_d[l].astype(F32), SSM_HEADDIM).reshape(1, -1), ssm_norm_g[l].reshape(1, -1))
        y_ml = _mlstm(h, hs, bsz, seq, _pad_row(jnp.concatenate([ml_b_i[l], ml_b_f[l]]), LANES),
                      ml_norm_g[l].reshape(1, -1))
        y_att = _dsa(h, hs, pos, bsz, seq, att_q_norm_g[l], att_w_uq[l], idx_k_ln_g[l], idx_k_ln_b[l])

        x1, top_i, gates = _mix_ln_router(
            (y_gm, y_ssm, y_ml, y_att), xf, w_out[l].astype(BF16), ln1_g[l].reshape(1, -1), ln1_b[l].reshape(1, -1),
            jnp.pad(router_w[l], ((0, 0), (0, LANES - N_EXPERTS))),
            _pad_row(router_b[l], LANES, -jnp.inf), alpha)

        slot, tile_e, n_used = _slot_positions(top_i[:, :TOP_K], n_tiles)
        xs = _dispatch(slot, x1, n_slots)
        act = _expert_up(tile_e, n_used, xs, expert_w_gu, expert_b_gu[:, :, None, :], l)
        eo = _expert_down(tile_e, n_used, act, expert_w_down, expert_b_down[:, :, None, :], l)
        xf, xb = _combine_ln(slot, x1, gates, ln2_g[l].reshape(1, -1), ln2_b[l].reshape(1, -1), eo, alpha)
    return xf.reshape(bsz, seq, d)
```
